```python
import math
import jax
import jax.numpy as jnp
from jax import lax
import numpy as np

D_MODEL = 1024
BATCH = 32
SEQ = 256
DEPTH = 2
DEC_BATCH = 4
DEC_SEQ = 4096
PAST_LEN = 256

GRID_W = 64
D_MIX = D_MODEL
D_HY = D_MIX // 2
D_ML = D_MIX - D_HY
ML_HEADS = 4
ML_HD = D_ML // ML_HEADS
CHUNK = 128
HY_BANDS = 16
HY_EMB = 2 * HY_BANDS + 1
HY_FFN = 64
HY_DECAY_TARGET = 1e-2
HY_FAST_DECAY_PCT = 0.3
HY_SLOW_DECAY_PCT = 1.5
HY_MIN_DECAY = math.log(HY_DECAY_TARGET) / HY_SLOW_DECAY_PCT
HY_MAX_DECAY = math.log(HY_DECAY_TARGET) / HY_FAST_DECAY_PCT
D_FF = 2816
N_GATES = 4 * ML_HEADS
N_IN = 3 * D_HY + 4 * D_ML + N_GATES
N_MOD = 6 * D_MODEL
EPS = 1e-6
F32 = jnp.float32

kernel_name = 'hyena_mlstm_prefix_diffusion_step'


def _rmsnorm(x, w):
    xf = x.astype(F32)
    y = xf * lax.rsqrt(jnp.mean(xf * xf, axis=-1, keepdims=True) + EPS)
    return (y * w.astype(F32)).astype(x.dtype)


def _dwconv1d(u, w, b):
    up = jnp.pad(u, ((0, 0), (1, 1), (0, 0)))
    return up[:, :-2] * w[0] + up[:, 1:-1] * w[1] + up[:, 2:] * w[2] + b


def _dwconv2d(u, w, b):
    y = lax.conv_general_dilated(u, w[:, :, None, :].astype(u.dtype), (1, 1), 'SAME',
                                 dimension_numbers=('NHWC', 'HWIO', 'NHWC'),
                                 feature_group_count=u.shape[-1])
    return y + b


def _hyena_kernel(L, w1, b1, w2, b2, w3, b3, w4, freq):
    t = jnp.linspace(0.0, 1.0, L, dtype=F32)[:, None]
    wpos = (2.0 * math.pi / L) * jnp.arange(L, dtype=F32)[:, None]
    bands = jnp.linspace(1e-4, HY_BANDS - 1, HY_BANDS, dtype=F32)[None, :]
    z = jnp.concatenate([t, jnp.cos(bands * wpos), -jnp.sin(bands * wpos)], axis=-1)
    fr = freq.astype(F32)
    h = jnp.sin(fr * (z @ w1.astype(F32) + b1.astype(F32)))
    h = jnp.sin(fr * (h @ w2.astype(F32) + b2.astype(F32)))
    h = jnp.sin(fr * (h @ w3.astype(F32) + b3.astype(F32)))
    h = h @ w4.astype(F32)
    deltas = jnp.linspace(HY_MIN_DECAY, HY_MAX_DECAY, D_HY, dtype=F32)
    decay = jnp.exp(-t * jnp.abs(deltas)[None, :])
    h = h * jnp.concatenate([decay, decay], axis=-1)
    hf, hb = h[:, :D_HY], h[:, D_HY:]
    k = jnp.concatenate([hf, jnp.zeros((1, D_HY), F32), hb[:0:-1]], axis=0)
    return k / jnp.sum(jnp.abs(k), axis=0, keepdims=True)


def _hyena(u, p):
    dt = u.dtype
    B, L, _ = u.shape
    u = _dwconv1d(u, p['hy_conv_w'], p['hy_conv_b']).astype(F32)
    x0, x1, v = jnp.split(u, 3, axis=-1)
    v = v * x1
    k = _hyena_kernel(L, *p['hy_filt'])
    n = 2 * L
    y = jnp.fft.irfft(jnp.fft.rfft(v, n=n, axis=1) * jnp.fft.rfft(k, n=n, axis=0)[None],
                      n=n, axis=1)[:, :L]
    y = (y + p['hy_bias'].astype(F32) * v) * x0
    return y.astype(dt)


def _mlstm_scan(q, k, v, i_pre, f_pre, C0, n0, m0):
    B, L, H, dh = q.shape
    nc = L // CHUNK

    def chunks(a):
        a = a.reshape((B, nc, CHUNK) + a.shape[2:])
        return jnp.moveaxis(jnp.moveaxis(a, 1, 0), 3, 2)

    causal = jnp.tril(jnp.ones((CHUNK, CHUNK), dtype=bool))

    def step(carry, xs):
        C, n, m = carry
        qc, kc, vc, ic, lfc = xs
        b = jnp.cumsum(lfc, axis=-1)
        dmat = jnp.where(causal, b[..., :, None] - b[..., None, :] + ic[..., None, :], -jnp.inf)
        inter = b + m[..., None]
        m_t = jnp.maximum(inter, jnp.max(dmat, axis=-1))
        s = jnp.einsum('bhtd,bhsd->bhts', qc, kc) * jnp.exp(dmat - m_t[..., None])
        w_inter = jnp.exp(inter - m_t)
        num = jnp.einsum('bhts,bhse->bhte', s, vc) + w_inter[..., None] * jnp.einsum('bhed,bhtd->bhte', C, qc)
        den = jnp.sum(s, axis=-1) + w_inter * jnp.einsum('bhd,bhtd->bht', n, qc)
        h = num / jnp.maximum(jnp.abs(den), jnp.exp(-m_t))[..., None]
        b_end = b[..., -1]
        w_log = b_end[..., None] - b + ic
        m_new = jnp.maximum(b_end + m, jnp.max(w_log, axis=-1))
        w_s = jnp.exp(w_log - m_new[..., None])
        w_c = jnp.exp(b_end + m - m_new)
        C_new = w_c[..., None, None] * C + jnp.einsum('bhs,bhse,bhsd->bhed', w_s, vc, kc)
        n_new = w_c[..., None] * n + jnp.einsum('bhs,bhsd->bhd', w_s, kc)
        return (C_new, n_new, m_new), h

    xs = (chunks(q), chunks(k), chunks(v), chunks(i_pre), chunks(jax.nn.log_sigmoid(f_pre)))
    (C, n, m), hs = lax.scan(step, (C0.astype(F32), n0.astype(F32), m0.astype(F32)), xs)
    h = jnp.swapaxes(jnp.moveaxis(hs, 0, 1), 2, 3).reshape(B, L, H, dh)
    return h, C, n, m


def _mlstm(u, g, p, C0, n0, m0):
    dt = u.dtype
    B, L, _ = u.shape
    qk = jax.nn.silu(_dwconv1d(u[..., :2 * D_ML], p['ml_conv_w'], p['ml_conv_b']).astype(F32))
    q = qk[..., :D_ML].reshape(B, L, ML_HEADS, ML_HD)
    k = qk[..., D_ML:].reshape(B, L, ML_HEADS, ML_HD) * (ML_HD ** -0.5)
    v = u[..., 2 * D_ML:3 * D_ML].astype(F32).reshape(B, L, ML_HEADS, ML_HD)
    o = u[..., 3 * D_ML:].astype(F32)
    i_f, f_f, i_b, f_b = jnp.split(g.astype(F32), 4, axis=-1)
    h_f, Cf, nf, mf = _mlstm_scan(q, k, v, i_f, f_f, C0[:, 0], n0[:, 0], m0[:, 0])
    h_b, Cb, nb, mb = _mlstm_scan(q[:, ::-1], k[:, ::-1], v[:, ::-1], i_b[:, ::-1], f_b[:, ::-1],
                                  C0[:, 1], n0[:, 1], m0[:, 1])
    h = h_f + h_b[:, ::-1]
    h = h * lax.rsqrt(jnp.mean(h * h, axis=-1, keepdims=True) + EPS)
    h = h * p['ml_norm_w'].astype(F32).reshape(ML_HEADS, ML_HD)
    y = h.reshape(B, L, D_ML) * jax.nn.sigmoid(o)
    return (y.astype(dt), jnp.stack([Cf, Cb], axis=1), jnp.stack([nf, nb], axis=1),
            jnp.stack([mf, mb], axis=1))


def _layer(x, mod, p, C0, n0, m0, rows):
    B, L, _ = x.shape
    shift_m, scale_m, gate_m, shift_f, scale_f, gate_f = jnp.split(mod, 6, axis=-1)
    h = _rmsnorm(x, p['norm_mix_w']) * (1 + scale_m) + shift_m
    proj = h @ p['w_in']
    u_hy = proj[..., :3 * D_HY]
    u_ml = proj[..., 3 * D_HY:3 * D_HY + 4 * D_ML]
    g_ml = proj[..., 3 * D_HY + 4 * D_ML:] + p['b_gate']
    y_hy = _hyena(u_hy, p)
    y_ml, C, n, m = _mlstm(u_ml, g_ml, p, C0, n0, m0)
    x = x + gate_m * (jnp.concatenate([y_hy, y_ml], axis=-1) @ p['w_out'])
    h = _rmsnorm(x, p['norm_ffn_w']) * (1 + scale_f) + shift_f
    up = (h @ p['ffn_w_up']).reshape(B, rows, L // rows, 2 * D_FF)
    up = _dwconv2d(up, p['ffn_conv_w'], p['ffn_conv_b']).reshape(B, L, 2 * D_FF)
    a, val = jnp.split(up, 2, axis=-1)
    x = x + gate_f * ((jax.nn.silu(a) * val) @ p['ffn_w_down'])
    return x, C, n, m


def setup_inputs(seed: int = 0) -> dict:
    key = jax.random.key(seed)
    ks = jax.random.split(key, 34)

    def nrm(i, shape, scale):
        return jax.random.normal(ks[i], shape, F32) * scale

    f_bias = jnp.linspace(3.0, 6.0, ML_HEADS, dtype=F32)
    z_h = jnp.zeros((ML_HEADS,), F32)
    gate_base = jnp.concatenate([z_h, f_bias, z_h, f_bias])
    return {
        'x_prompt': nrm(0, (BATCH, SEQ, D_MODEL), 1.0),
        'x_sample': nrm(1, (DEC_BATCH, DEC_SEQ, D_MODEL), 1.0),
        'state_mlstm_C': nrm(2, (DEC_BATCH, DEPTH, 2, ML_HEADS, ML_HD, ML_HD), 0.1),
        'state_mlstm_n': nrm(3, (DEC_BATCH, DEPTH, 2, ML_HEADS, ML_HD), 0.5),
        'state_mlstm_m': nrm(4, (DEC_BATCH, DEPTH, 2, ML_HEADS), 0.5),
        'c': nrm(5, (DEC_BATCH, D_MODEL), 1.0),
        'c_ctx': nrm(6, (D_MODEL,), 1.0),
        'ada_w': nrm(7, (DEPTH, D_MODEL, N_MOD), 0.5 * D_MODEL ** -0.5),
        'ada_b': nrm(8, (DEPTH, N_MOD), 0.02),
        'norm_mix_w': 1.0 + nrm(9, (DEPTH, D_MODEL), 0.05),
        'w_in': nrm(10, (DEPTH, D_MODEL, N_IN), D_MODEL ** -0.5),
        'b_gate': gate_base[None, :] + nrm(11, (DEPTH, N_GATES), 0.1),
        'hy_conv_w': nrm(12, (DEPTH, 3, 3 * D_HY), 3 ** -0.5),
        'hy_conv_b': nrm(13, (DEPTH, 3 * D_HY), 0.02),
        'hy_filt_w1': nrm(14, (DEPTH, HY_EMB, HY_FFN), HY_EMB ** -0.5),
        'hy_filt_b1': nrm(15, (DEPTH, HY_FFN), 0.1),
        'hy_filt_w2': nrm(16, (DEPTH, HY_FFN, HY_FFN), HY_FFN ** -0.5),
        'hy_filt_b2': nrm(17, (DEPTH, HY_FFN), 0.1),
        'hy_filt_w3': nrm(18, (DEPTH, HY_FFN, HY_FFN), HY_FFN ** -0.5),
        'hy_filt_b3': nrm(19, (DEPTH, HY_FFN), 0.1),
        'hy_filt_w4': nrm(20, (DEPTH, HY_FFN, 2 * D_HY), HY_FFN ** -0.5),
        'hy_freq': 1.0 + nrm(21, (DEPTH, HY_FFN), 0.1),
        'hy_bias': nrm(22, (DEPTH, D_HY), 1.0),
        'ml_conv_w': nrm(23, (DEPTH, 3, 2 * D_ML), 3 ** -0.5),
        'ml_conv_b': nrm(24, (DEPTH, 2 * D_ML), 0.02),
        'ml_norm_w': 1.0 + nrm(25, (DEPTH, D_ML), 0.05),
        'w_out': nrm(26, (DEPTH, D_MIX, D_MODEL), D_MIX ** -0.5),
        'norm_ffn_w': 1.0 + nrm(27, (DEPTH, D_MODEL), 0.05),
        'ffn_w_up': nrm(28, (DEPTH, D_MODEL, 2 * D_FF), D_MODEL ** -0.5),
        'ffn_conv_w': nrm(29, (DEPTH, 3, 3, 2 * D_FF), 1.0 / 3.0),
        'ffn_conv_b': nrm(30, (DEPTH, 2 * D_FF), 0.02),
        'ffn_w_down': nrm(31, (DEPTH, D_FF, D_MODEL), D_FF ** -0.5),
        'final_norm_w': 1.0 + nrm(32, (D_MODEL,), 0.05),
    }


def reference(x_prompt, x_sample, state_mlstm_C, state_mlstm_n, state_mlstm_m, c, c_ctx,
              ada_w, ada_b, norm_mix_w, w_in, b_gate, hy_conv_w, hy_conv_b,
              hy_filt_w1, hy_filt_b1, hy_filt_w2, hy_filt_b2, hy_filt_w3, hy_filt_b3, hy_filt_w4,
              hy_freq, hy_bias, ml_conv_w, ml_conv_b, ml_norm_w, w_out, norm_ffn_w,
              ffn_w_up, ffn_conv_w, ffn_conv_b, ffn_w_down, final_norm_w):
    rows_lat = x_sample.shape[1] // GRID_W
    bp = x_prompt.shape[0]
    zC = jnp.zeros((bp, 2, ML_HEADS, ML_HD, ML_HD), F32)
    zn = jnp.zeros((bp, 2, ML_HEADS, ML_HD), F32)
    zm = jnp.zeros((bp, 2, ML_HEADS), F32)
    xp, xs = x_prompt, x_sample
    new_C, new_n, new_m = [], [], []
    for l in range(DEPTH):
        p = {
            'norm_mix_w': norm_mix_w[l], 'w_in': w_in[l], 'b_gate': b_gate[l],
            'hy_conv_w': hy_conv_w[l], 'hy_conv_b': hy_conv_b[l],
            'hy_filt': (hy_filt_w1[l], hy_filt_b1[l], hy_filt_w2[l], hy_filt_b2[l],
                        hy_filt_w3[l], hy_filt_b3[l], hy_filt_w4[l], hy_freq[l]),
            'hy_bias': hy_bias[l], 'ml_conv_w': ml_conv_w[l], 'ml_conv_b': ml_conv_b[l],
            'ml_norm_w': ml_norm_w[l], 'w_out': w_out[l], 'norm_ffn_w': norm_ffn_w[l],
            'ffn_w_up': ffn_w_up[l], 'ffn_conv_w': ffn_conv_w[l], 'ffn_conv_b': ffn_conv_b[l],
            'ffn_w_down': ffn_w_down[l],
        }
        mod_ctx = (jax.nn.silu(c_ctx) @ ada_w[l] + ada_b[l])[None, None, :]
        mod_lat = (jax.nn.silu(c) @ ada_w[l] + ada_b[l])[:, None, :]
        xp, Cl, nl, ml = _layer(xp, mod_ctx, p, zC, zn, zm, 1)
        new_C.append(Cl)
        new_n.append(nl)
        new_m.append(ml)
        xs, _, _, _ = _layer(xs, mod_lat, p, state_mlstm_C[:, l], state_mlstm_n[:, l],
                             state_mlstm_m[:, l], rows_lat)
    y_prompt = _rmsnorm(xp, final_norm_w)
    y_sample = _rmsnorm(xs, final_norm_w)
    new_mlstm_C = jnp.stack(new_C, axis=1)
    new_mlstm_n = jnp.stack(new_n, axis=1)
    new_mlstm_m = jnp.stack(new_m, axis=1)
    return (y_prompt, y_sample, new_mlstm_C, new_mlstm_n, new_mlstm_m)
```

```python
import functools
import math

import numpy as np
import jax
import jax.numpy as jnp
from jax import lax
from jax.experimental import pallas as pl
from jax.experimental.pallas import tpu as pltpu

F32 = jnp.float32
BF16 = jnp.bfloat16
HIGHEST = lax.Precision.HIGHEST

GRID_W = 64
ML_HEADS = 4
CHUNK = 128
HY_BANDS = 16
HY_DECAY_TARGET = 1e-2
HY_FAST_DECAY_PCT = 0.3
HY_SLOW_DECAY_PCT = 1.5
HY_MIN_DECAY = math.log(HY_DECAY_TARGET) / HY_SLOW_DECAY_PCT
HY_MAX_DECAY = math.log(HY_DECAY_TARGET) / HY_FAST_DECAY_PCT
EPS = 1e-6

LANE = 128
VMEM_LIMIT = 56 * 1024 * 1024
MOD_ROWS = 8
DFT_ROW_TILE = 64


def _params(*sem):
    return pltpu.CompilerParams(dimension_semantics=sem, vmem_limit_bytes=VMEM_LIMIT)


def _silu(x):
    return x * jax.nn.sigmoid(x)


def _rms(x, w):
    return x * lax.rsqrt(jnp.mean(x * x, axis=-1, keepdims=True) + EPS) * w


def _ada_kernel(c_ref, w_ref, b_ref, o_ref):
    s = _silu(c_ref[...])
    o_ref[0] = jnp.dot(s, w_ref[0], precision=HIGHEST, preferred_element_type=F32) + b_ref[0]


def _ada(cvec, ada_w, ada_b):
    depth, d, nmod = ada_w.shape
    tn = 1024
    return pl.pallas_call(
        _ada_kernel,
        grid=(depth, nmod // tn),
        in_specs=[
            pl.BlockSpec((MOD_ROWS, d), lambda l, j: (0, 0)),
            pl.BlockSpec((1, d, tn), lambda l, j: (l, 0, j)),
            pl.BlockSpec((1, 1, tn), lambda l, j: (l, 0, j)),
        ],
        out_specs=pl.BlockSpec((1, MOD_ROWS, tn), lambda l, j: (l, 0, j)),
        out_shape=jax.ShapeDtypeStruct((depth, MOD_ROWS, nmod), F32),
        compiler_params=_params("parallel", "parallel"),
        name="ada_mod",
    )(cvec, ada_w, ada_b.reshape(depth, 1, nmod))


def _inproj_kernel(x_ref, mod_ref, nw_ref, why_ref, wml_ref, wg_ref, bg_ref,
                   uhy_ref, uml_ref, g_ref):
    m = mod_ref[0]
    h = _rms(x_ref[0], nw_ref[...]) * (1.0 + m[1:2]) + m[0:1]
    hb = h.astype(BF16)
    uhy_ref[0] = jnp.dot(hb, why_ref[...], preferred_element_type=F32)
    uml_ref[0] = jnp.dot(hb, wml_ref[...], preferred_element_type=F32)
    g_ref[0] = jnp.dot(h, wg_ref[...], precision=HIGHEST, preferred_element_type=F32) + bg_ref[...]


def _inproj(x, mod, mod_row, nw, w_hy, w_ml, w_g, b_g):
    bsz, seq, d = x.shape
    tb = 256
    n_hy, n_ml = w_hy.shape[1], w_ml.shape[1]
    tok = lambda n: pl.BlockSpec((1, tb, n), lambda b, i: (b, i, 0))
    full = lambda a: pl.BlockSpec(a.shape, lambda b, i: (0,) * a.ndim)
    return pl.pallas_call(
        _inproj_kernel,
        grid=(bsz, seq // tb),
        in_specs=[tok(d), pl.BlockSpec((1, 6, d), lambda b, i: (mod_row(b), 0, 0)),
                  full(nw), full(w_hy), full(w_ml), full(w_g), full(b_g)],
        out_specs=[tok(n_hy), tok(n_ml), tok(LANE)],
        out_shape=[jax.ShapeDtypeStruct((bsz, seq, n_hy), F32),
                   jax.ShapeDtypeStruct((bsz, seq, n_ml), F32),
                   jax.ShapeDtypeStruct((bsz, seq, LANE), F32)],
        compiler_params=_params("parallel", "parallel"),
        name="inproj",
    )(x, mod, nw, w_hy, w_ml, w_g, b_g)


def _dwconv3(u, w, b):
    n = u.shape[0]
    row = lax.broadcasted_iota(jnp.int32, u.shape, 0)
    prev = jnp.where(row == 0, 0.0, pltpu.roll(u, 1, 0))
    nxt = jnp.where(row == n - 1, 0.0, pltpu.roll(u, n - 1, 0))
    return prev * w[0:1] + u * w[1:2] + nxt * w[2:3] + b


def _hy_pre_kernel(u0_ref, u1_ref, u2_ref, w0_ref, w1_ref, w2_ref, b0_ref, b1_ref, b2_ref,
                   v_ref, x0_ref):
    x0_ref[0] = _dwconv3(u0_ref[0], w0_ref[...], b0_ref[...])
    x1 = _dwconv3(u1_ref[0], w1_ref[...], b1_ref[...])
    v_ref[0] = _dwconv3(u2_ref[0], w2_ref[...], b2_ref[...]) * x1


def _hy_pre(u_hy, conv_w, conv_b):
    bsz, seq, c3 = u_hy.shape
    nblk = c3 // 3 // LANE
    u_spec = lambda k: pl.BlockSpec((1, seq, LANE), lambda b, j: (b, 0, k * nblk + j))
    w_spec = lambda k: pl.BlockSpec((3, LANE), lambda b, j: (0, k * nblk + j))
    b_spec = lambda k: pl.BlockSpec((1, LANE), lambda b, j: (0, k * nblk + j))
    out = pl.BlockSpec((1, seq, LANE), lambda b, j: (b, 0, j))
    shp = jax.ShapeDtypeStruct((bsz, seq, c3 // 3), F32)
    return pl.pallas_call(
        _hy_pre_kernel,
        grid=(bsz, nblk),
        in_specs=[u_spec(0), u_spec(1), u_spec(2), w_spec(0), w_spec(1), w_spec(2),
                  b_spec(0), b_spec(1), b_spec(2)],
        out_specs=[out, out],
        out_shape=[shp, shp],
        compiler_params=_params("parallel", "parallel"),
        name="hyena_pre",
    )(u_hy, u_hy, u_hy, conv_w, conv_w, conv_w, conv_b, conv_b, conv_b)


def _ml_pre_kernel(u_ref, w_ref, b_ref, o_ref, *, nq, kscale):
    y = _silu(_dwconv3(u_ref[0], w_ref[...], b_ref[...]))
    scale = jnp.where(pl.program_id(1) >= nq, kscale, 1.0)
    o_ref[0] = (y * scale).astype(o_ref.dtype)


def _ml_pre(u_ml, conv_w, conv_b, d_ml, dh):
    bsz, seq, _ = u_ml.shape
    nblk = 2 * d_ml // LANE
    return pl.pallas_call(
        functools.partial(_ml_pre_kernel, nq=d_ml // LANE, kscale=dh ** -0.5),
        grid=(bsz, nblk),
        in_specs=[pl.BlockSpec((1, seq, LANE), lambda b, j: (b, 0, j)),
                  pl.BlockSpec((3, LANE), lambda b, j: (0, j)),
                  pl.BlockSpec((1, LANE), lambda b, j: (0, j))],
        out_specs=pl.BlockSpec((1, seq, LANE), lambda b, j: (b, 0, j)),
        out_shape=jax.ShapeDtypeStruct((bsz, seq, 2 * d_ml), BF16),
        compiler_params=_params("parallel", "parallel"),
        name="mlstm_pre",
    )(u_ml, conv_w, conv_b)


def _dft_table_kernel(pre_ref, pim_ref, qre_ref, qim_ref, c_ref, s_ref, st_ref):
    pre, pim = pre_ref[0], pim_ref[0]
    qre, qim = qre_ref[...], qim_ref[...]
    c = pre * qre - pim * qim
    s = pre * qim + pim * qre
    grow = lax.broadcasted_iota(jnp.int32, c.shape, 0) + pl.program_id(0) * c.shape[0]
    lane = lax.broadcasted_iota(jnp.int32, c.shape, 1)
    alt_lane = (1 - 2 * (lane & 1)).astype(F32)
    alt_row = (1 - 2 * (grow & 1)).astype(F32)
    c_ref[...] = c.astype(BF16)
    s_ref[...] = jnp.where(grow == 0, alt_lane, s).astype(BF16)
    st_ref[...] = jnp.where(lane == 0, alt_row, s).astype(BF16)


def _dft_tables(seq):
    n = 2 * seq
    tm = DFT_ROW_TILE
    s = jnp.arange(seq, dtype=jnp.int32)[None, :]
    ang = lambda f: (-2.0 * math.pi / n) * ((f * s) % n).astype(F32)
    fp = (jnp.arange(seq // tm, dtype=jnp.int32) * tm)[:, None]
    fq = jnp.arange(tm, dtype=jnp.int32)[:, None]
    ap, aq = ang(fp), ang(fq)
    p3 = lambda a: a.reshape(seq // tm, 1, seq)
    pspec = pl.BlockSpec((1, 1, seq), lambda r: (r, 0, 0))
    qspec = pl.BlockSpec((tm, seq), lambda r: (0, 0))
    out = pl.BlockSpec((tm, seq), lambda r: (r, 0))
    shp = jax.ShapeDtypeStruct((seq, seq), BF16)
    return pl.pallas_call(
        _dft_table_kernel,
        grid=(seq // tm,),
        in_specs=[pspec, pspec, qspec, qspec],
        out_specs=[out, out, out],
        out_shape=[shp, shp, shp],
        compiler_params=_params("parallel"),
        name="dft_tables",
    )(p3(jnp.cos(ap)), p3(jnp.sin(ap)), jnp.cos(aq), jnp.sin(aq))


def _filter_kernel(z_ref, w1_ref, b1_ref, w2_ref, b2_ref, w3_ref, b3_ref, fr_ref,
                   w4f_ref, w4b_ref, dl_ref, a_ref, d_ref, ny_ref):
    z = z_ref[...]
    fr = fr_ref[...]
    dot = functools.partial(jnp.dot, precision=HIGHEST, preferred_element_type=F32)
    h = jnp.sin(fr * (dot(z, w1_ref[...]) + b1_ref[...]))
    h = jnp.sin(fr * (dot(h, w2_ref[...]) + b2_ref[...]))
    h = jnp.sin(fr * (dot(h, w3_ref[...]) + b3_ref[...]))
    decay = jnp.exp(-z[:, 0:1] * dl_ref[...])
    hf = dot(h, w4f_ref[...]) * decay
    hb = dot(h, w4b_ref[...]) * decay
    row = lax.broadcasted_iota(jnp.int32, hf.shape, 0)
    hb = jnp.where(row == 0, 0.0, hb)
    inv = 1.0 / jnp.sum(jnp.abs(hf) + jnp.abs(hb), axis=0, keepdims=True)
    a = (hf + hb) * inv
    a_ref[...] = a.astype(BF16)
    d_ref[...] = ((hf - hb) * inv).astype(BF16)
    ny_ref[...] = jnp.sum(a * (1 - 2 * (row & 1)).astype(F32), axis=0, keepdims=True)


def _filter_taps(seq, filt, d_hy):
    w1, b1, w2, b2, w3, b3, w4, freq = filt
    emb, ffn = w1.shape
    t = jnp.linspace(0.0, 1.0, seq, dtype=F32)[:, None]
    wpos = (2.0 * math.pi / seq) * jnp.arange(seq, dtype=F32)[:, None]
    bands = jnp.linspace(1e-4, HY_BANDS - 1, HY_BANDS, dtype=F32)[None, :]
    z = jnp.concatenate([t, jnp.cos(bands * wpos), -jnp.sin(bands * wpos),
                         jnp.zeros((seq, LANE - emb), F32)], axis=-1)
    w1p = jnp.concatenate([w1, jnp.zeros((LANE - emb, ffn), F32)], axis=0)
    deltas = jnp.abs(jnp.linspace(HY_MIN_DECAY, HY_MAX_DECAY, d_hy, dtype=F32))[None, :]
    row = lambda a: a.reshape(1, -1)
    nblk = d_hy // LANE
    full = lambda a: pl.BlockSpec(a.shape, lambda j: (0,) * a.ndim)
    args = [z, w1p, row(b1), w2, row(b2), w3, row(b3), row(freq)]
    col = pl.BlockSpec((seq, LANE), lambda j: (0, j))
    vec = pl.BlockSpec((1, LANE), lambda j: (0, j))
    return pl.pallas_call(
        _filter_kernel,
        grid=(nblk,),
        in_specs=[full(a) for a in args] + [
            pl.BlockSpec((ffn, LANE), lambda j: (0, j)),
            pl.BlockSpec((ffn, LANE), lambda j: (0, nblk + j)), vec],
        out_specs=[col, col, vec],
        out_shape=[jax.ShapeDtypeStruct((seq, d_hy), BF16),
                   jax.ShapeDtypeStruct((seq, d_hy), BF16),
                   jax.ShapeDtypeStruct((1, d_hy), F32)],
        compiler_params=_params("parallel"),
        name="hyena_filter",
    )(*args, w4, w4, deltas)


def _filter_dft_kernel(c_ref, s_ref, a_ref, d_ref, ny_ref, kre_ref, kim_ref, are_ref, aim_ref):
    k = pl.program_id(1)

    @pl.when(k == 0)
    def _():
        are_ref[...] = jnp.zeros_like(are_ref)
        aim_ref[...] = jnp.zeros_like(aim_ref)

    are_ref[...] += jnp.dot(c_ref[...], a_ref[...], preferred_element_type=F32)
    aim_ref[...] += jnp.dot(s_ref[...], d_ref[...], preferred_element_type=F32)

    @pl.when(k == pl.num_programs(1) - 1)
    def _():
        grow = lax.broadcasted_iota(jnp.int32, are_ref.shape, 0) + pl.program_id(0) * are_ref.shape[0]
        kre_ref[...] = are_ref[...]
        kim_ref[...] = jnp.where(grow == 0, ny_ref[...], aim_ref[...])


def _filter_dft(ctab, stab, a, d, ny):
    seq, c = a.shape
    t = min(seq, 512)
    return pl.pallas_call(
        _filter_dft_kernel,
        grid=(seq // t, seq // t),
        in_specs=[pl.BlockSpec((t, t), lambda m, k: (m, k)),
                  pl.BlockSpec((t, t), lambda m, k: (m, k)),
                  pl.BlockSpec((t, c), lambda m, k: (k, 0)),
                  pl.BlockSpec((t, c), lambda m, k: (k, 0)),
                  pl.BlockSpec((1, c), lambda m, k: (0, 0))],
        out_specs=[pl.BlockSpec((t, c), lambda m, k: (m, 0))] * 2,
        out_shape=[jax.ShapeDtypeStruct((seq, c), F32)] * 2,
        scratch_shapes=[pltpu.VMEM((t, c), F32), pltpu.VMEM((t, c), F32)],
        compiler_params=_params("parallel", "arbitrary"),
        name="hyena_filter_dft",
    )(ctab, stab, a, d, ny)


def _dft_fwd_kernel(c_ref, s_ref, v_ref, kre_ref, kim_ref, yre_ref, yim_ref, are_ref, aim_ref):
    k = pl.program_id(2)
    bt = v_ref.shape[0]

    @pl.when(k == 0)
    def _():
        are_ref[...] = jnp.zeros_like(are_ref)
        aim_ref[...] = jnp.zeros_like(aim_ref)

    for b in range(bt):
        vb = v_ref[b].astype(BF16)
        are_ref[b] += jnp.dot(c_ref[...], vb, preferred_element_type=F32)
        aim_ref[b] += jnp.dot(s_ref[...], vb, preferred_element_type=F32)

    @pl.when(k == pl.num_programs(2) - 1)
    def _():
        kre, kim = kre_ref[...], kim_ref[...]
        grow = lax.broadcasted_iota(jnp.int32, kre.shape, 0) + pl.program_id(1) * kre.shape[0]
        packed = grow == 0
        for b in range(bt):
            xr, xi = are_ref[b], aim_ref[b]
            yre = jnp.where(packed, 0.5 * xr * kre, xr * kre - xi * kim)
            yim = jnp.where(packed, 0.5 * xi * kim, xr * kim + xi * kre)
            yre_ref[b] = yre.astype(BF16)
            yim_ref[b] = yim.astype(BF16)


def _dft_fwd(ctab, stab, v, kre, kim, bt):
    bsz, seq, c = v.shape
    t = min(seq, 512)
    tab = pl.BlockSpec((t, t), lambda b, m, k: (m, k))
    kspec = pl.BlockSpec((t, c), lambda b, m, k: (m, 0))
    out = pl.BlockSpec((bt, t, c), lambda b, m, k: (b, m, 0))
    shp = jax.ShapeDtypeStruct((bsz, seq, c), BF16)
    return pl.pallas_call(
        _dft_fwd_kernel,
        grid=(bsz // bt, seq // t, seq // t),
        in_specs=[tab, tab, pl.BlockSpec((bt, t, c), lambda b, m, k: (b, k, 0)), kspec, kspec],
        out_specs=[out, out],
        out_shape=[shp, shp],
        scratch_shapes=[pltpu.VMEM((bt, t, c), F32), pltpu.VMEM((bt, t, c), F32)],
        compiler_params=_params("parallel", "parallel", "arbitrary"),
        name="hyena_dft_fwd",
    )(ctab, stab, v, kre, kim)


def _dft_inv_kernel(c_ref, st_ref, yre_ref, yim_ref, v_ref, x0_ref, hb_ref, o_ref, acc_ref, *, scale):
    k = pl.program_id(2)
    bt = v_ref.shape[0]

    @pl.when(k == 0)
    def _():
        acc_ref[...] = jnp.zeros_like(acc_ref)

    for b in range(bt):
        acc_ref[b] += (jnp.dot(c_ref[...], yre_ref[b], preferred_element_type=F32)
                       + jnp.dot(st_ref[...], yim_ref[b], preferred_element_type=F32))

    @pl.when(k == pl.num_programs(2) - 1)
    def _():
        for b in range(bt):
            v = v_ref[b]
            o_ref[b] = (acc_ref[b] * scale + hb_ref[...] * v) * x0_ref[b]


def _dft_inv(ctab, sttab, yre, yim, v, x0, hy_bias, bt):
    bsz, seq, c = v.shape
    t = min(seq, 512)
    tab = pl.BlockSpec((t, t), lambda b, m, k: (m, k))
    yspec = pl.BlockSpec((bt, t, c), lambda b, m, k: (b, k, 0))
    tok = pl.BlockSpec((bt, t, c), lambda b, m, k: (b, m, 0))
    return pl.pallas_call(
        functools.partial(_dft_inv_kernel, scale=1.0 / seq),
        grid=(bsz // bt, seq // t, seq // t),
        in_specs=[tab, tab, yspec, yspec, tok, tok, pl.BlockSpec((1, c), lambda b, m, k: (0, 0))],
        out_specs=tok,
        out_shape=jax.ShapeDtypeStruct((bsz, seq, c), F32),
        scratch_shapes=[pltpu.VMEM((bt, t, c), F32)],
        compiler_params=_params("parallel", "parallel", "arbitrary"),
        name="hyena_dft_inv",
    )(ctab, sttab, yre, yim, v, x0, hy_bias)


def _log_sigmoid(x):
    return jnp.minimum(x, 0.0) - jnp.log1p(jnp.exp(-jnp.abs(x)))


def _mlstm_kernel(q_ref, k_ref, v_ref, g_ref, gt_ref, c0_ref, n0_ref, m0_ref,
                  h_ref, c_out_ref, n_out_ref, m_out_ref, c_s, n_s, m_s, *, heads, dh):
    step = pl.program_id(2)
    fwd = pl.program_id(0) == 0
    t = q_ref.shape[1]

    @pl.when(step == 0)
    def _():
        c_s[...] = c0_ref[0, 0]
        n_s[...] = n0_ref[0, 0]
        m_s[...] = m0_ref[0, 0]

    row = lax.broadcasted_iota(jnp.int32, (t, t), 0)
    col = lax.broadcasted_iota(jnp.int32, (t, t), 1)
    sgn = jnp.where(fwd, 1, -1)
    seen = sgn * (row - col) >= 0
    seen_t = sgn * (col - row) >= 0
    g = g_ref[0, 0]
    gt = gt_ref[0, 0]
    hdot = functools.partial(jnp.dot, precision=HIGHEST, preferred_element_type=F32)
    lf = _log_sigmoid(g)
    bcol_all = hdot(seen.astype(F32), lf)
    brow_all = hdot(_log_sigmoid(gt), seen_t.astype(F32))
    q = q_ref[0]
    k = k_ref[0]
    v = v_ref[0]
    nt = (((1,), (1,)), ((), ()))
    tn = (((0,), (0,)), ((), ()))
    for hd in range(heads):
        sl = slice(hd * dh, (hd + 1) * dh)
        fcol = heads + hd
        bc = bcol_all[:, fcol:fcol + 1]
        br = brow_all[fcol:fcol + 1, :]
        ic = g[:, hd:hd + 1]
        ir = gt[hd:hd + 1, :]
        m_prev = m_s[hd:hd + 1, 0:1]
        c_prev = c_s[hd]
        n_prev = n_s[hd:hd + 1, :]
        qh, kh = q[:, sl], k[:, sl]
        vh = v[:, sl]
        dmat = jnp.where(seen, bc - br + ir, -jnp.inf)
        inter = bc + m_prev
        m_t = jnp.maximum(inter, jnp.max(dmat, axis=1, keepdims=True))
        s = lax.dot_general(qh, kh, nt, preferred_element_type=F32) * jnp.exp(dmat - m_t)
        w_inter = jnp.exp(inter - m_t)
        num = (jnp.dot(s.astype(BF16), vh.astype(BF16), preferred_element_type=F32)
               + w_inter * lax.dot_general(qh, c_prev.astype(BF16), nt, preferred_element_type=F32))
        qn = jnp.sum(qh.astype(F32) * n_prev, axis=1, keepdims=True)
        den = jnp.sum(s, axis=1, keepdims=True) + w_inter * qn
        h_ref[0, 0, :, sl] = num / jnp.maximum(jnp.abs(den), jnp.exp(-m_t))
        b_end = jnp.sum(lf[:, fcol:fcol + 1], axis=0, keepdims=True)
        w_log = b_end - bc + ic
        m_new = jnp.maximum(b_end + m_prev, jnp.max(w_log, axis=0, keepdims=True))
        w_s = jnp.exp(w_log - m_new)
        w_c = jnp.exp(b_end + m_prev - m_new)
        c_new = w_c * c_prev + lax.dot_general((vh * w_s).astype(BF16), kh, tn,
                                               preferred_element_type=F32)
        n_new = w_c * n_prev + jnp.sum(kh.astype(F32) * w_s, axis=0, keepdims=True)
        c_s[hd] = c_new
        n_s[hd:hd + 1, :] = n_new
        m_s[hd:hd + 1, :] = jnp.broadcast_to(m_new, (1, dh))

    @pl.when(step == pl.num_programs(2) - 1)
    def _():
        c_out_ref[0, 0] = c_s[...]
        n_out_ref[0, 0] = n_s[...]
        m_out_ref[0, 0] = m_s[...]


def _mlstm(qk, u_ml, g2, gt2, c0, n0, m0, d_ml):
    bsz, seq, _ = qk.shape
    heads = ML_HEADS
    dh = d_ml // heads
    nc = seq // CHUNK
    chunk = lambda d, c: c + d * (nc - 1 - 2 * c)
    tok = lambda j: pl.BlockSpec((1, CHUNK, d_ml), lambda d, b, c: (b, chunk(d, c), j))
    st = lambda *tail: pl.BlockSpec((1, 1, heads) + tail, lambda d, b, c: (b, d, 0) + (0,) * len(tail))
    return pl.pallas_call(
        functools.partial(_mlstm_kernel, heads=heads, dh=dh),
        grid=(2, bsz, nc),
        in_specs=[tok(0), tok(1), tok(2),
                  pl.BlockSpec((1, 1, CHUNK, LANE), lambda d, b, c: (d, b, chunk(d, c), 0)),
                  pl.BlockSpec((1, 1, 2 * heads, CHUNK), lambda d, b, c: (d, b, 0, chunk(d, c))),
                  st(dh, dh), st(dh), st(dh)],
        out_specs=[pl.BlockSpec((1, 1, CHUNK, d_ml), lambda d, b, c: (d, b, chunk(d, c), 0)),
                   st(dh, dh), st(dh), st(dh)],
        out_shape=[jax.ShapeDtypeStruct((2, bsz, seq, d_ml), F32),
                   jax.ShapeDtypeStruct((bsz, 2, heads, dh, dh), F32),
                   jax.ShapeDtypeStruct((bsz, 2, heads, dh), F32),
                   jax.ShapeDtypeStruct((bsz, 2, heads, dh), F32)],
        scratch_shapes=[pltpu.VMEM((heads, dh, dh), F32), pltpu.VMEM((heads, dh), F32),
                        pltpu.VMEM((heads, dh), F32)],
        compiler_params=_params("parallel", "parallel", "arbitrary"),
        name="mlstm_scan",
    )(qk, qk, u_ml, g2, gt2, c0, n0, m0)


def _mix_out_kernel(yhy_ref, hf_ref, hb_ref, o_ref, x_ref, mod_ref, mlw_ref, wout_ref, fnw_ref,
                    x1_ref, h2_ref, *, heads, dh):
    m = mod_ref[0]
    h = hf_ref[0, 0] + hb_ref[0, 0]
    o = o_ref[0]
    mlw = mlw_ref[...]
    parts = [yhy_ref[0].astype(BF16)]
    for hd in range(heads):
        sl = slice(hd * dh, (hd + 1) * dh)
        parts.append((_rms(h[:, sl], mlw[:, sl]) * jax.nn.sigmoid(o[:, sl])).astype(BF16))
    y = jnp.concatenate(parts, axis=-1)
    x1 = x_ref[0] + m[2:3] * jnp.dot(y, wout_ref[...], preferred_element_type=F32)
    x1_ref[0] = x1
    h2_ref[0] = (_rms(x1, fnw_ref[...]) * (1.0 + m[4:5]) + m[3:4]).astype(BF16)


def _mix_out(y_hy, h2dir, u_ml, x, mod, mod_row, ml_norm_w, w_out, norm_ffn_w):
    bsz, seq, d = x.shape
    d_ml = h2dir.shape[-1]
    tb = 256
    tok = lambda n, j=0: pl.BlockSpec((1, tb, n), lambda b, i: (b, i, j))
    hdir = lambda dr: pl.BlockSpec((1, 1, tb, d_ml), lambda b, i: (dr, b, i, 0))
    full = lambda a: pl.BlockSpec(a.shape, lambda b, i: (0,) * a.ndim)
    return pl.pallas_call(
        functools.partial(_mix_out_kernel, heads=ML_HEADS, dh=d_ml // ML_HEADS),
        grid=(bsz, seq // tb),
        in_specs=[tok(y_hy.shape[-1]), hdir(0), hdir(1), tok(d_ml, 3), tok(d),
                  pl.BlockSpec((1, 6, d), lambda b, i: (mod_row(b), 0, 0)),
                  full(ml_norm_w), full(w_out), full(norm_ffn_w)],
        out_specs=[tok(d), tok(d)],
        out_shape=[jax.ShapeDtypeStruct((bsz, seq, d), F32),
                   jax.ShapeDtypeStruct((bsz, seq, d), BF16)],
        compiler_params=_params("parallel", "parallel"),
        name="mixer_out",
    )(y_hy, h2dir, h2dir, u_ml, x, mod, ml_norm_w, w_out, norm_ffn_w)


def _ffn_kernel(*refs, width, halo, final):
    if halo:
        (top_ref, main_ref, bot_ref, x1_ref, mod_ref, wa_ref, wv_ref, cwa_ref, cwv_ref,
         cba_ref, cbv_ref, wd_ref, fw_ref, o_ref, acc_ref) = refs
    else:
        (main_ref, x1_ref, mod_ref, wa_ref, wv_ref, cwa_ref, cwv_ref,
         cba_ref, cbv_ref, wd_ref, fw_ref, o_ref, acc_ref) = refs
    j = pl.program_id(2)
    d = main_ref.shape[-1]
    hm = main_ref[...].reshape(-1, d)
    m_rows = hm.shape[0]

    @pl.when(j == 0)
    def _():
        acc_ref[...] = jnp.zeros_like(acc_ref)

    col = lax.broadcasted_iota(jnp.int32, (m_rows, 1), 0) % width
    not_first = (col != 0).astype(F32)
    not_last = (col != width - 1).astype(F32)

    def conv(w_ref, cw_ref, cb_ref):
        w = w_ref[...]
        cw = cw_ref[...]
        um = jnp.dot(hm, w, preferred_element_type=F32)
        slabs = [(1, um)]
        if halo:
            r = pl.program_id(1)
            top_ok = jnp.where(r > 0, 1.0, 0.0)
            bot_ok = jnp.where(r < pl.num_programs(1) - 1, 1.0, 0.0)
            ut = jnp.dot(top_ref[0], w, preferred_element_type=F32) * top_ok
            ub = jnp.dot(bot_ref[0], w, preferred_element_type=F32) * bot_ok
            full = jnp.concatenate([ut, um, ub], axis=0)
            slabs = [(kr, full[kr * width:kr * width + m_rows]) for kr in range(3)]
        left = sum(s * cw[3 * kr:3 * kr + 1] for kr, s in slabs)
        mid = sum(s * cw[3 * kr + 1:3 * kr + 2] for kr, s in slabs)
        right = sum(s * cw[3 * kr + 2:3 * kr + 3] for kr, s in slabs)
        return (mid + not_first * pltpu.roll(left, 1, 0)
                + not_last * pltpu.roll(right, m_rows - 1, 0) + cb_ref[...])

    a = conv(wa_ref, cwa_ref, cba_ref)
    val = conv(wv_ref, cwv_ref, cbv_ref)
    act = (_silu(a) * val).astype(BF16)
    acc_ref[...] += jnp.dot(act, wd_ref[...], preferred_element_type=F32)

    @pl.when(j == pl.num_programs(2) - 1)
    def _():
        m = mod_ref[0]
        x2 = x1_ref[...].reshape(-1, d) + m[5:6] * acc_ref[...]
        if final:
            x2 = _rms(x2, fw_ref[...])
        o_ref[...] = x2.reshape(o_ref.shape)


def _ffn(h2, x1, mod, mod_row, w_up, conv_w, conv_b, w_down, final_w, rows, final):
    bsz, seq, d = x1.shape
    d_ff = w_down.shape[0]
    width = seq // rows
    tn = 256
    nj = d_ff // tn
    halo = rows > 1
    if halo:
        rb = 16
        nb, m_tok, nr = 1, rb * width, rows // rb
    else:
        nb, m_tok, nr = math.gcd(bsz, 4), seq, 1
    tok = pl.BlockSpec((nb, m_tok, d), lambda b, r, j: (b, r, 0))
    wcol = lambda k, n0: pl.BlockSpec((k, tn), lambda b, r, j: (0, n0 + j))
    in_specs, args = [], []
    if halo:
        in_specs += [pl.BlockSpec((1, width, d), lambda b, r, j: (b, jnp.maximum(r * rb - 1, 0), 0)),
                     tok,
                     pl.BlockSpec((1, width, d), lambda b, r, j: (b, jnp.minimum((r + 1) * rb, rows - 1), 0))]
        args += [h2, h2, h2]
    else:
        in_specs += [tok]
        args += [h2]
    in_specs += [tok, pl.BlockSpec((1, 6, d), lambda b, r, j: (mod_row(b), 0, 0)),
                 wcol(d, 0), wcol(d, nj), wcol(9, 0), wcol(9, nj), wcol(1, 0), wcol(1, nj),
                 pl.BlockSpec((tn, d), lambda b, r, j: (j, 0)),
                 pl.BlockSpec((1, d), lambda b, r, j: (0, 0))]
    args += [x1, mod, w_up, w_up, conv_w, conv_w, conv_b, conv_b, w_down, final_w]
    return pl.pallas_call(
        functools.partial(_ffn_kernel, width=width, halo=halo, final=final),
        grid=(bsz // nb, nr, nj),
        in_specs=in_specs,
        out_specs=tok,
        out_shape=jax.ShapeDtypeStruct((bsz, seq, d), F32),
        scratch_shapes=[pltpu.VMEM((nb * m_tok, d), F32)],
        compiler_params=_params("parallel", "parallel", "arbitrary"),
        name="conv_ffn",
    )(*args)


def _layer(x, mod, mod_row, p, tabs, c0, n0, m0, rows, final):
    bsz, seq, d = x.shape
    d_hy, d_ml = p["d_hy"], p["d_ml"]
    heads = ML_HEADS
    ctab, stab, sttab = tabs
    u_hy, u_ml, g = _inproj(x, mod, mod_row, p["norm_mix_w"], p["w_hy"], p["w_ml"], p["w_g"], p["b_g"])
    v, x0 = _hy_pre(u_hy, p["hy_conv_w"], p["hy_conv_b"])
    a, dd, ny = _filter_taps(seq, p["hy_filt"], d_hy)
    kre, kim = _filter_dft(ctab, stab, a, dd, ny)
    bt = math.gcd(bsz, 4)
    yre, yim = _dft_fwd(ctab, stab, v, kre, kim, bt)
    y_hy = _dft_inv(ctab, sttab, yre, yim, v, x0, p["hy_bias"], bt)
    qk = _ml_pre(u_ml, p["ml_conv_w"], p["ml_conv_b"], d_ml, d_ml // heads)
    gi = g[..., :4 * heads].reshape(bsz, seq, 2, 2 * heads)
    g2 = jnp.pad(jnp.moveaxis(gi, 2, 0), ((0, 0), (0, 0), (0, 0), (0, LANE - 2 * heads)))
    gt2 = jnp.transpose(gi, (2, 0, 3, 1))
    m0b = jnp.broadcast_to(m0[..., None], m0.shape + (d_ml // heads,))
    h2dir, c_new, n_new, m_new = _mlstm(qk, u_ml, g2, gt2, c0, n0, m0b, d_ml)
    x1, h2 = _mix_out(y_hy, h2dir, u_ml, x, mod, mod_row, p["ml_norm_w"], p["w_out"], p["norm_ffn_w"])
    x2 = _ffn(h2, x1, mod, mod_row, p["ffn_w_up"], p["ffn_conv_w"], p["ffn_conv_b"],
              p["ffn_w_down"], p["final_norm_w"], rows, final)
    return x2, c_new, n_new, m_new[..., 0]


def kernel(x_prompt, x_sample, state_mlstm_C, state_mlstm_n, state_mlstm_m, c, c_ctx, ada_w, ada_b, norm_mix_w, w_in, b_gate, hy_conv_w, hy_conv_b, hy_filt_w1, hy_filt_b1, hy_filt_w2, hy_filt_b2, hy_filt_w3, hy_filt_b3, hy_filt_w4, hy_freq, hy_bias, ml_conv_w, ml_conv_b, ml_norm_w, w_out, norm_ffn_w, ffn_w_up, ffn_conv_w, ffn_conv_b, ffn_w_down, final_norm_w):
    depth, d, _ = ada_w.shape
    bp, seq_p, _ = x_prompt.shape
    bs, seq_s, _ = x_sample.shape
    heads = ML_HEADS
    d_hy = hy_bias.shape[-1]
    d_ml = ml_norm_w.shape[-1]
    dh = d_ml // heads
    n_gates = b_gate.shape[-1]
    d_ff = ffn_w_down.shape[1]
    assert bs < MOD_ROWS and seq_s % GRID_W == 0

    cvec = jnp.concatenate([c, c_ctx[None, :], jnp.zeros((MOD_ROWS - bs - 1, d), F32)], axis=0)
    mod_all = _ada(cvec, ada_w, ada_b).reshape(depth, MOD_ROWS, 6, d)
    row_ctx = lambda b: bs
    row_lat = lambda b: b

    tabs_p = _dft_tables(seq_p)
    tabs_s = _dft_tables(seq_s)
    zc = jnp.zeros((bp, 2, heads, dh, dh), F32)
    zn = jnp.zeros((bp, 2, heads, dh), F32)
    zm = jnp.zeros((bp, 2, heads), F32)

    xp, xs = x_prompt, x_sample
    new_c, new_n, new_m = [], [], []
    for l in range(depth):
        w = w_in[l]
        p = {
            "d_hy": d_hy, "d_ml": d_ml,
            "norm_mix_w": norm_mix_w[l][None, :],
            "w_hy": w[:, :3 * d_hy].astype(BF16),
            "w_ml": w[:, 3 * d_hy:3 * d_hy + 4 * d_ml].astype(BF16),
            "w_g": jnp.pad(w[:, 3 * d_hy + 4 * d_ml:], ((0, 0), (0, LANE - n_gates))),
            "b_g": jnp.pad(b_gate[l], (0, LANE - n_gates))[None, :],
            "hy_conv_w": hy_conv_w[l], "hy_conv_b": hy_conv_b[l][None, :],
            "hy_filt": (hy_filt_w1[l], hy_filt_b1[l], hy_filt_w2[l], hy_filt_b2[l],
                        hy_filt_w3[l], hy_filt_b3[l], hy_filt_w4[l], hy_freq[l]),
            "hy_bias": hy_bias[l][None, :],
            "ml_conv_w": ml_conv_w[l], "ml_conv_b": ml_conv_b[l][None, :],
            "ml_norm_w": ml_norm_w[l][None, :],
            "w_out": w_out[l].astype(BF16),
            "norm_ffn_w": norm_ffn_w[l][None, :],
            "ffn_w_up": ffn_w_up[l].astype(BF16),
            "ffn_conv_w": ffn_conv_w[l].reshape(9, 2 * d_ff),
            "ffn_conv_b": ffn_conv_b[l][None, :],
            "ffn_w_down": ffn_w_down[l].astype(BF16),
            "final_norm_w": final_norm_w[None, :],
        }
        final = l == depth - 1
        xp, cl, nl, ml = _layer(xp, mod_all[l], row_ctx, p, tabs_p, zc, zn, zm, 1, final)
        new_c.append(cl)
        new_n.append(nl)
        new_m.append(ml)
        xs, _, _, _ = _layer(xs, mod_all[l], row_lat, p, tabs_s, state_mlstm_C[:, l],
                             state_mlstm_n[:, l], state_mlstm_m[:, l], seq_s // GRID_W, final)
    return (xp, xs, jnp.stack(new_c, axis=1), jnp.stack(new_n, axis=1), jnp.stack(new_m, axis=1))
```

```python
import functools
import math

import numpy as np
import jax
import jax.numpy as jnp
from jax import lax
from jax.experimental import pallas as pl
from jax.experimental.pallas import tpu as pltpu

F32 = jnp.float32
BF16 = jnp.bfloat16
HIGHEST = lax.Precision.HIGHEST

GRID_W = 64
ML_HEADS = 4
CHUNK = 128
HY_BANDS = 16
HY_DECAY_TARGET = 1e-2
HY_FAST_DECAY_PCT = 0.3
HY_SLOW_DECAY_PCT = 1.5
HY_MIN_DECAY = math.log(HY_DECAY_TARGET) / HY_SLOW_DECAY_PCT
HY_MAX_DECAY = math.log(HY_DECAY_TARGET) / HY_FAST_DECAY_PCT
EPS = 1e-6

LANE = 128
SUBLANE = 8
VMEM_LIMIT = 56 * 1024 * 1024
MOD_ROWS = 8
DFT_ROW_TILE = 64


def _params(*sem):
    return pltpu.CompilerParams(dimension_semantics=sem, vmem_limit_bytes=VMEM_LIMIT)


def _silu(x):
    return x * jax.nn.sigmoid(x)


def _rms(x, w):
    return x * lax.rsqrt(jnp.mean(x * x, axis=-1, keepdims=True) + EPS) * w


def _split_hi_lo(x):
    hi = x.astype(BF16)
    return hi, (x - hi.astype(F32)).astype(BF16)


def _ada_kernel(c_ref, w_ref, b_ref, o_ref):
    s = _silu(c_ref[...])
    o_ref[0] = jnp.dot(s, w_ref[0], precision=HIGHEST, preferred_element_type=F32) + b_ref[0]


def _ada(cvec, ada_w, ada_b):
    depth, d, nmod = ada_w.shape
    tn = 1024
    return pl.pallas_call(
        _ada_kernel,
        grid=(depth, nmod // tn),
        in_specs=[
            pl.BlockSpec((MOD_ROWS, d), lambda l, j: (0, 0)),
            pl.BlockSpec((1, d, tn), lambda l, j: (l, 0, j)),
            pl.BlockSpec((1, 1, tn), lambda l, j: (l, 0, j)),
        ],
        out_specs=pl.BlockSpec((1, MOD_ROWS, tn), lambda l, j: (l, 0, j)),
        out_shape=jax.ShapeDtypeStruct((depth, MOD_ROWS, nmod), F32),
        compiler_params=_params("parallel", "parallel"),
        name="ada_mod",
    )(cvec, ada_w, ada_b.reshape(depth, 1, nmod))


def _inproj_kernel(x_ref, mod_ref, nw_ref, why_ref, wml_ref, wg_ref, bg_ref,
                   uhy_ref, uml_ref, g_ref):
    m = mod_ref[0]
    h = _rms(x_ref[0], nw_ref[...]) * (1.0 + m[1:2]) + m[0:1]
    h_hi, h_lo = _split_hi_lo(h)
    uhy_ref[0] = jnp.dot(h_hi, why_ref[...], preferred_element_type=F32)
    uml_ref[0] = jnp.dot(h_hi, wml_ref[...], preferred_element_type=F32)
    wg = wg_ref[...]
    ng = g_ref.shape[-1]
    a = jnp.dot(h_hi, wg, preferred_element_type=F32)
    b = jnp.dot(h_lo, wg[:, :ng], preferred_element_type=F32)
    g_ref[0] = a[:, :ng] + a[:, ng:] + b + bg_ref[...]


def _inproj(x, mod, mod_row, nw, w_hy, w_ml, w_g, b_g):
    bsz, seq, d = x.shape
    tb = 256
    n_hy, n_ml = w_hy.shape[1], w_ml.shape[1]
    tok = lambda n: pl.BlockSpec((1, tb, n), lambda b, i: (b, i, 0))
    full = lambda a: pl.BlockSpec(a.shape, lambda b, i: (0,) * a.ndim)
    return pl.pallas_call(
        _inproj_kernel,
        grid=(bsz, seq // tb),
        in_specs=[tok(d), pl.BlockSpec((1, 6, d), lambda b, i: (mod_row(b), 0, 0)),
                  full(nw), full(w_hy), full(w_ml), full(w_g), full(b_g)],
        out_specs=[tok(n_hy), tok(n_ml), tok(LANE)],
        out_shape=[jax.ShapeDtypeStruct((bsz, seq, n_hy), F32),
                   jax.ShapeDtypeStruct((bsz, seq, n_ml), F32),
                   jax.ShapeDtypeStruct((bsz, seq, LANE), F32)],
        compiler_params=_params("parallel", "parallel"),
        name="inproj",
    )(x, mod, nw, w_hy, w_ml, w_g, b_g)


def _dwconv3(u, before, after, w, b):
    n = u.shape[0]
    row = lax.broadcasted_iota(jnp.int32, u.shape, 0)
    prev = jnp.where(row == 0, before, pltpu.roll(u, 1, 0))
    nxt = jnp.where(row == n - 1, after, pltpu.roll(u, n - 1, 0))
    return prev * w[0:1] + u * w[1:2] + nxt * w[2:3] + b


def _tile_neighbours(prev_ref, next_ref):
    i = pl.program_id(1)
    before = jnp.where(i == 0, 0.0, prev_ref[0, SUBLANE - 1:SUBLANE, :])
    after = jnp.where(i == pl.num_programs(1) - 1, 0.0, next_ref[0, 0:1, :])
    return before, after


def _row_tile_specs(seq, tb, c, cblk=0):
    per = tb // SUBLANE
    last = seq // SUBLANE - 1
    main = pl.BlockSpec((1, tb, c), lambda b, i: (b, i, cblk))
    prev = pl.BlockSpec((1, SUBLANE, c), lambda b, i: (b, jnp.maximum(i * per - 1, 0), cblk))
    nxt = pl.BlockSpec((1, SUBLANE, c), lambda b, i: (b, jnp.minimum((i + 1) * per, last), cblk))
    return main, prev, nxt


def _hy_pre_kernel(u_ref, up_ref, un_ref, w_ref, b_ref, v_ref, x0_ref):
    before, after = _tile_neighbours(up_ref, un_ref)
    y = _dwconv3(u_ref[0], before, after, w_ref[...], b_ref[...])
    c = v_ref.shape[-1]
    x0_ref[0] = y[:, :c]
    v_ref[0] = y[:, 2 * c:] * y[:, c:2 * c]


def _hy_pre(u_hy, conv_w, conv_b):
    bsz, seq, c3 = u_hy.shape
    tb = min(seq, 512)
    main, prev, nxt = _row_tile_specs(seq, tb, c3)
    full = lambda a: pl.BlockSpec(a.shape, lambda b, i: (0,) * a.ndim)
    out = pl.BlockSpec((1, tb, c3 // 3), lambda b, i: (b, i, 0))
    shp = jax.ShapeDtypeStruct((bsz, seq, c3 // 3), F32)
    return pl.pallas_call(
        _hy_pre_kernel,
        grid=(bsz, seq // tb),
        in_specs=[main, prev, nxt, full(conv_w), full(conv_b)],
        out_specs=[out, out],
        out_shape=[shp, shp],
        compiler_params=_params("parallel", "parallel"),
        name="hyena_pre",
    )(u_hy, u_hy, u_hy, conv_w, conv_b)


def _ml_pre_kernel(u_ref, up_ref, un_ref, w_ref, b_ref, o_ref, *, d_ml, kscale):
    before, after = _tile_neighbours(up_ref, un_ref)
    y = _silu(_dwconv3(u_ref[0], before, after, w_ref[...], b_ref[...]))
    lane = lax.broadcasted_iota(jnp.int32, (1, y.shape[1]), 1)
    o_ref[0] = (y * jnp.where(lane >= d_ml, kscale, 1.0)).astype(o_ref.dtype)


def _ml_pre(u_ml, conv_w, conv_b, d_ml, dh):
    bsz, seq, _ = u_ml.shape
    tb = min(seq, 512)
    main, prev, nxt = _row_tile_specs(seq, tb, 2 * d_ml)
    full = lambda a: pl.BlockSpec(a.shape, lambda b, i: (0,) * a.ndim)
    return pl.pallas_call(
        functools.partial(_ml_pre_kernel, d_ml=d_ml, kscale=dh ** -0.5),
        grid=(bsz, seq // tb),
        in_specs=[main, prev, nxt, full(conv_w), full(conv_b)],
        out_specs=pl.BlockSpec((1, tb, 2 * d_ml), lambda b, i: (b, i, 0)),
        out_shape=jax.ShapeDtypeStruct((bsz, seq, 2 * d_ml), BF16),
        compiler_params=_params("parallel", "parallel"),
        name="mlstm_pre",
    )(u_ml, u_ml, u_ml, conv_w, conv_b)


def _dft_table_kernel(pre_ref, pim_ref, qre_ref, qim_ref, c_ref, s_ref, st_ref):
    pre, pim = pre_ref[0], pim_ref[0]
    qre, qim = qre_ref[...], qim_ref[...]
    c = pre * qre - pim * qim
    s = pre * qim + pim * qre
    grow = lax.broadcasted_iota(jnp.int32, c.shape, 0) + pl.program_id(0) * c.shape[0]
    lane = lax.broadcasted_iota(jnp.int32, c.shape, 1)
    alt_lane = (1 - 2 * (lane & 1)).astype(F32)
    alt_row = (1 - 2 * (grow & 1)).astype(F32)
    c_ref[...] = c.astype(BF16)
    s_ref[...] = jnp.where(grow == 0, alt_lane, s).astype(BF16)
    st_ref[...] = jnp.where(lane == 0, alt_row, s).astype(BF16)


def _dft_tables(seq):
    n = 2 * seq
    tm = DFT_ROW_TILE
    s = jnp.arange(seq, dtype=jnp.int32)[None, :]
    ang = lambda f: (-2.0 * math.pi / n) * ((f * s) % n).astype(F32)
    fp = (jnp.arange(seq // tm, dtype=jnp.int32) * tm)[:, None]
    fq = jnp.arange(tm, dtype=jnp.int32)[:, None]
    ap, aq = ang(fp), ang(fq)
    p3 = lambda a: a.reshape(seq // tm, 1, seq)
    pspec = pl.BlockSpec((1, 1, seq), lambda r: (r, 0, 0))
    qspec = pl.BlockSpec((tm, seq), lambda r: (0, 0))
    out = pl.BlockSpec((tm, seq), lambda r: (r, 0))
    shp = jax.ShapeDtypeStruct((seq, seq), BF16)
    return pl.pallas_call(
        _dft_table_kernel,
        grid=(seq // tm,),
        in_specs=[pspec, pspec, qspec, qspec],
        out_specs=[out, out, out],
        out_shape=[shp, shp, shp],
        compiler_params=_params("parallel"),
        name="dft_tables",
    )(p3(jnp.cos(ap)), p3(jnp.sin(ap)), jnp.cos(aq), jnp.sin(aq))


def _filter_kernel(z_ref, w1_ref, b1_ref, w2_ref, b2_ref, w3_ref, b3_ref, fr_ref,
                   w4f_ref, w4b_ref, dl_ref, a_ref, d_ref, ny_ref, h_ref):
    dot = functools.partial(jnp.dot, precision=HIGHEST, preferred_element_type=F32)

    @pl.when(pl.program_id(0) == 0)
    def _():
        fr = fr_ref[...]
        h = jnp.sin(fr * (dot(z_ref[...], w1_ref[...]) + b1_ref[...]))
        h = jnp.sin(fr * (dot(h, w2_ref[...]) + b2_ref[...]))
        h_ref[...] = jnp.sin(fr * (dot(h, w3_ref[...]) + b3_ref[...]))

    h = h_ref[...]
    decay = jnp.exp(-z_ref[:, 0:1] * dl_ref[...])
    hf = dot(h, w4f_ref[...]) * decay
    hb = dot(h, w4b_ref[...]) * decay
    row = lax.broadcasted_iota(jnp.int32, hf.shape, 0)
    hb = jnp.where(row == 0, 0.0, hb)
    inv = 1.0 / jnp.sum(jnp.abs(hf) + jnp.abs(hb), axis=0, keepdims=True)
    a = (hf + hb) * inv
    a_ref[...] = a.astype(BF16)
    d_ref[...] = ((hf - hb) * inv).astype(BF16)
    ny_ref[...] = jnp.sum(a * (1 - 2 * (row & 1)).astype(F32), axis=0, keepdims=True)


def _filter_taps(seq, filt, d_hy):
    w1, b1, w2, b2, w3, b3, w4, freq = filt
    emb, ffn = w1.shape
    t = jnp.linspace(0.0, 1.0, seq, dtype=F32)[:, None]
    wpos = (2.0 * math.pi / seq) * jnp.arange(seq, dtype=F32)[:, None]
    bands = jnp.linspace(1e-4, HY_BANDS - 1, HY_BANDS, dtype=F32)[None, :]
    z = jnp.concatenate([t, jnp.cos(bands * wpos), -jnp.sin(bands * wpos),
                         jnp.zeros((seq, LANE - emb), F32)], axis=-1)
    w1p = jnp.concatenate([w1, jnp.zeros((LANE - emb, ffn), F32)], axis=0)
    deltas = jnp.abs(jnp.linspace(HY_MIN_DECAY, HY_MAX_DECAY, d_hy, dtype=F32))[None, :]
    row = lambda a: a.reshape(1, -1)
    nblk = d_hy // LANE
    full = lambda a: pl.BlockSpec(a.shape, lambda j: (0,) * a.ndim)
    args = [z, w1p, row(b1), w2, row(b2), w3, row(b3), row(freq)]
    col = pl.BlockSpec((seq, LANE), lambda j: (0, j))
    vec = pl.BlockSpec((1, LANE), lambda j: (0, j))
    return pl.pallas_call(
        _filter_kernel,
        grid=(nblk,),
        in_specs=[full(a) for a in args] + [
            pl.BlockSpec((ffn, LANE), lambda j: (0, j)),
            pl.BlockSpec((ffn, LANE), lambda j: (0, nblk + j)), vec],
        out_specs=[col, col, vec],
        out_shape=[jax.ShapeDtypeStruct((seq, d_hy), BF16),
                   jax.ShapeDtypeStruct((seq, d_hy), BF16),
                   jax.ShapeDtypeStruct((1, d_hy), F32)],
        scratch_shapes=[pltpu.VMEM((seq, ffn), F32)],
        compiler_params=_params("arbitrary"),
        name="hyena_filter",
    )(*args, w4, w4, deltas)


def _filter_dft_kernel(c_ref, s_ref, a_ref, d_ref, ny_ref, kre_ref, kim_ref, are_ref, aim_ref):
    k = pl.program_id(1)

    @pl.when(k == 0)
    def _():
        are_ref[...] = jnp.zeros_like(are_ref)
        aim_ref[...] = jnp.zeros_like(aim_ref)

    are_ref[...] += jnp.dot(c_ref[...], a_ref[...], preferred_element_type=F32)
    aim_ref[...] += jnp.dot(s_ref[...], d_ref[...], preferred_element_type=F32)

    @pl.when(k == pl.num_programs(1) - 1)
    def _():
        grow = lax.broadcasted_iota(jnp.int32, are_ref.shape, 0) + pl.program_id(0) * are_ref.shape[0]
        kre_ref[...] = are_ref[...]
        kim_ref[...] = jnp.where(grow == 0, ny_ref[...], aim_ref[...])


def _filter_dft(ctab, stab, a, d, ny):
    seq, c = a.shape
    t = min(seq, 512)
    return pl.pallas_call(
        _filter_dft_kernel,
        grid=(seq // t, seq // t),
        in_specs=[pl.BlockSpec((t, t), lambda m, k: (m, k)),
                  pl.BlockSpec((t, t), lambda m, k: (m, k)),
                  pl.BlockSpec((t, c), lambda m, k: (k, 0)),
                  pl.BlockSpec((t, c), lambda m, k: (k, 0)),
                  pl.BlockSpec((1, c), lambda m, k: (0, 0))],
        out_specs=[pl.BlockSpec((t, c), lambda m, k: (m, 0))] * 2,
        out_shape=[jax.ShapeDtypeStruct((seq, c), F32)] * 2,
        scratch_shapes=[pltpu.VMEM((t, c), F32), pltpu.VMEM((t, c), F32)],
        compiler_params=_params("parallel", "arbitrary"),
        name="hyena_filter_dft",
    )(ctab, stab, a, d, ny)


def _dft_fwd_kernel(c_ref, s_ref, v_ref, kre_ref, kim_ref, yre_ref, yim_ref, are_ref, aim_ref):
    k = pl.program_id(2)
    bt = v_ref.shape[0]

    @pl.when(k == 0)
    def _():
        are_ref[...] = jnp.zeros_like(are_ref)
        aim_ref[...] = jnp.zeros_like(aim_ref)

    for b in range(bt):
        vb = v_ref[b].astype(BF16)
        are_ref[b] += jnp.dot(c_ref[...], vb, preferred_element_type=F32)
        aim_ref[b] += jnp.dot(s_ref[...], vb, preferred_element_type=F32)

    @pl.when(k == pl.num_programs(2) - 1)
    def _():
        kre, kim = kre_ref[...], kim_ref[...]
        grow = lax.broadcasted_iota(jnp.int32, kre.shape, 0) + pl.program_id(1) * kre.shape[0]
        packed = grow == 0
        for b in range(bt):
            xr, xi = are_ref[b], aim_ref[b]
            yre = jnp.where(packed, 0.5 * xr * kre, xr * kre - xi * kim)
            yim = jnp.where(packed, 0.5 * xi * kim, xr * kim + xi * kre)
            yre_ref[b] = yre.astype(BF16)
            yim_ref[b] = yim.astype(BF16)


def _dft_fwd(ctab, stab, v, kre, kim, bt):
    bsz, seq, c = v.shape
    t = min(seq, 512)
    tab = pl.BlockSpec((t, t), lambda b, m, k: (m, k))
    kspec = pl.BlockSpec((t, c), lambda b, m, k: (m, 0))
    out = pl.BlockSpec((bt, t, c), lambda b, m, k: (b, m, 0))
    shp = jax.ShapeDtypeStruct((bsz, seq, c), BF16)
    return pl.pallas_call(
        _dft_fwd_kernel,
        grid=(bsz // bt, seq // t, seq // t),
        in_specs=[tab, tab, pl.BlockSpec((bt, t, c), lambda b, m, k: (b, k, 0)), kspec, kspec],
        out_specs=[out, out],
        out_shape=[shp, shp],
        scratch_shapes=[pltpu.VMEM((bt, t, c), F32), pltpu.VMEM((bt, t, c), F32)],
        compiler_params=_params("parallel", "parallel", "arbitrary"),
        name="hyena_dft_fwd",
    )(ctab, stab, v, kre, kim)


def _dft_inv_kernel(c_ref, st_ref, yre_ref, yim_ref, v_ref, x0_ref, hb_ref, o_ref, acc_ref, *, scale):
    k = pl.program_id(2)
    bt = v_ref.shape[0]

    @pl.when(k == 0)
    def _():
        acc_ref[...] = jnp.zeros_like(acc_ref)

    for b in range(bt):
        acc_ref[b] += (jnp.dot(c_ref[...], yre_ref[b], preferred_element_type=F32)
                       + jnp.dot(st_ref[...], yim_ref[b], preferred_element_type=F32))

    @pl.when(k == pl.num_programs(2) - 1)
    def _():
        for b in range(bt):
            v = v_ref[b]
            o_ref[b] = (acc_ref[b] * scale + hb_ref[...] * v) * x0_ref[b]


def _dft_inv(ctab, sttab, yre, yim, v, x0, hy_bias, bt):
    bsz, seq, c = v.shape
    t = min(seq, 512)
    tab = pl.BlockSpec((t, t), lambda b, m, k: (m, k))
    yspec = pl.BlockSpec((bt, t, c), lambda b, m, k: (b, k, 0))
    tok = pl.BlockSpec((bt, t, c), lambda b, m, k: (b, m, 0))
    return pl.pallas_call(
        functools.partial(_dft_inv_kernel, scale=1.0 / seq),
        grid=(bsz // bt, seq // t, seq // t),
        in_specs=[tab, tab, yspec, yspec, tok, tok, pl.BlockSpec((1, c), lambda b, m, k: (0, 0))],
        out_specs=tok,
        out_shape=jax.ShapeDtypeStruct((bsz, seq, c), F32),
        scratch_shapes=[pltpu.VMEM((bt, t, c), F32)],
        compiler_params=_params("parallel", "parallel", "arbitrary"),
        name="hyena_dft_inv",
    )(ctab, sttab, yre, yim, v, x0, hy_bias)


def _log_sigmoid(x):
    return jnp.minimum(x, 0.0) - jnp.log1p(jnp.exp(-jnp.abs(x)))


def _mlstm_kernel(q_ref, k_ref, v_ref, g_ref, gt_ref, c0_ref, n0_ref, m0_ref,
                  h_ref, c_out_ref, n_out_ref, m_out_ref, c_s, n_s, m_s, *, heads, dh):
    step = pl.program_id(2)
    fwd = pl.program_id(0) == 0
    bt, t = q_ref.shape[0], q_ref.shape[1]

    @pl.when(step == 0)
    def _():
        c_s[...] = c0_ref[:, 0]
        n_s[...] = n0_ref[:, 0]
        m_s[...] = m0_ref[:, 0]

    row = lax.broadcasted_iota(jnp.int32, (t, t), 0)
    col = lax.broadcasted_iota(jnp.int32, (t, t), 1)
    sgn = jnp.where(fwd, 1, -1)
    seen = sgn * (row - col) >= 0
    seen_b = seen.astype(BF16)
    seen_tb = (sgn * (col - row) >= 0).astype(BF16)
    nt = (((1,), (1,)), ((), ()))
    tn = (((0,), (0,)), ((), ()))

    def split3(x, axis):
        hi = x.astype(BF16).astype(F32)
        r = x - hi
        mid = r.astype(BF16).astype(F32)
        return jnp.concatenate([hi, mid, r - mid], axis=axis).astype(BF16)

    g = g_ref[0, 0]
    gt = gt_ref[0, 0]
    lf = _log_sigmoid(g)
    lf_tot = jnp.sum(lf, axis=0, keepdims=True)
    cs = jnp.dot(seen_b, split3(lf, 1), preferred_element_type=F32)
    nl = lf.shape[1]
    bcol_all = cs[:, :nl] + cs[:, nl:2 * nl] + cs[:, 2 * nl:]
    rs = jnp.dot(split3(_log_sigmoid(gt), 0), seen_tb, preferred_element_type=F32)
    nr = gt.shape[0]
    brow_all = rs[:nr] + rs[nr:2 * nr] + rs[2 * nr:]
    chains = [(b, hd) for b in range(bt) for hd in range(heads)]
    sl = lambda hd: slice(hd * dh, (hd + 1) * dh)
    qs = [q_ref[b, :, sl(hd)] for b, hd in chains]
    ks = [k_ref[b, :, sl(hd)] for b, hd in chains]
    vs = [v_ref[b, :, sl(hd)].astype(BF16) for b, hd in chains]
    qk = [lax.dot_general(q, k, nt, preferred_element_type=F32) for q, k in zip(qs, ks)]
    each = lambda f, *ls: [f(*a) for a in zip(*ls)]
    fcols = [b * 2 * heads + heads + hd for b, hd in chains]
    bc = [bcol_all[:, c:c + 1] for c in fcols]
    rowt = [gt[c - heads:c - heads + 1, :] - brow_all[c:c + 1, :] for c in fcols]
    ic = [g[:, c - heads:c - heads + 1] for c in fcols]
    b_end = [lf_tot[:, c:c + 1] for c in fcols]
    m_prev = [m_s[b, hd:hd + 1, 0:1] for b, hd in chains]
    ct_prev = [c_s[b, hd] for b, hd in chains]
    n_prev = [n_s[b, hd:hd + 1, :] for b, hd in chains]
    dmat = each(lambda x, y: jnp.where(seen, x + y, -jnp.inf), bc, rowt)
    inter = each(lambda x, m: x + m, bc, m_prev)
    dmax = each(lambda d: jnp.max(d, axis=1, keepdims=True), dmat)
    m_t = each(jnp.maximum, inter, dmax)
    s = each(lambda r, d, m: r * jnp.exp(d - m), qk, dmat, m_t)
    w_inter = each(lambda i, m: jnp.exp(i - m), inter, m_t)
    qf = each(lambda q: q.astype(F32), qs)
    lhs = each(lambda x, q, w: jnp.concatenate([x.astype(BF16), (q * w).astype(BF16)], axis=1),
               s, qf, w_inter)
    rhs = each(lambda v, c: jnp.concatenate([v, c.astype(BF16)], axis=0), vs, ct_prev)
    nums = each(lambda l, r: jnp.dot(l, r, preferred_element_type=F32), lhs, rhs)
    qn = each(lambda q, n: jnp.sum(q * n, axis=1, keepdims=True), qf, n_prev)
    ssum = each(lambda x: jnp.sum(x, axis=1, keepdims=True), s)
    den = each(lambda a, w, x: a + w * x, ssum, w_inter, qn)
    w_log = each(lambda e, x, i: e - x + i, b_end, bc, ic)
    wmax = each(lambda w: jnp.max(w, axis=0, keepdims=True), w_log)
    m_new = each(lambda e, m, w: jnp.maximum(e + m, w), b_end, m_prev, wmax)
    w_s = each(lambda w, m: jnp.exp(w - m), w_log, m_new)
    w_c = each(lambda e, m, mn: jnp.exp(e + m - mn), b_end, m_prev, m_new)
    kw = each(lambda k, w: k.astype(F32) * w, ks, w_s)
    upds = each(lambda x, v: lax.dot_general(x.astype(BF16), v, tn, preferred_element_type=F32), kw, vs)
    ksum = each(lambda x: jnp.sum(x, axis=0, keepdims=True), kw)
    for i, (b, hd) in enumerate(chains):
        h_ref[0, b, :, sl(hd)] = nums[i] / jnp.maximum(jnp.abs(den[i]), jnp.exp(-m_t[i]))
        c_s[b, hd] = w_c[i] * ct_prev[i] + upds[i]
        n_s[b, hd:hd + 1, :] = w_c[i] * n_prev[i] + ksum[i]
        m_s[b, hd:hd + 1, :] = jnp.broadcast_to(m_new[i], (1, dh))

    @pl.when(step == pl.num_programs(2) - 1)
    def _():
        c_out_ref[:, 0] = c_s[...]
        n_out_ref[:, 0] = n_s[...]
        m_out_ref[:, 0] = m_s[...]


def _mlstm(qk, u_ml, g, c0, n0, m0, d_ml):
    bsz, seq, _ = qk.shape
    heads = ML_HEADS
    dh = d_ml // heads
    nc = seq // CHUNK
    bt = math.gcd(bsz, 4)
    nbb = bsz // bt
    gi = g[..., :4 * heads].reshape(nbb, bt, seq, 2, 2 * heads)
    g2 = jnp.transpose(gi, (3, 0, 2, 1, 4)).reshape(2, nbb, seq, bt * 2 * heads)
    gt2 = jnp.swapaxes(g2, 2, 3)
    g2 = jnp.pad(g2, ((0, 0), (0, 0), (0, 0), (0, LANE - bt * 2 * heads)))
    chunk = lambda d, c: c + d * (nc - 1 - 2 * c)
    tok = lambda j: pl.BlockSpec((bt, CHUNK, d_ml), lambda d, b, c: (b, chunk(d, c), j))
    st = lambda *tail: pl.BlockSpec((bt, 1, heads) + tail, lambda d, b, c: (b, d, 0) + (0,) * len(tail))
    h, ct, n, m = pl.pallas_call(
        functools.partial(_mlstm_kernel, heads=heads, dh=dh),
        grid=(2, nbb, nc),
        in_specs=[tok(0), tok(1), tok(2),
                  pl.BlockSpec((1, 1, CHUNK, LANE), lambda d, b, c: (d, b, chunk(d, c), 0)),
                  pl.BlockSpec((1, 1, bt * 2 * heads, CHUNK), lambda d, b, c: (d, b, 0, chunk(d, c))),
                  st(dh, dh), st(dh), st(dh)],
        out_specs=[pl.BlockSpec((1, bt, CHUNK, d_ml), lambda d, b, c: (d, b, chunk(d, c), 0)),
                   st(dh, dh), st(dh), st(dh)],
        out_shape=[jax.ShapeDtypeStruct((2, bsz, seq, d_ml), F32),
                   jax.ShapeDtypeStruct((bsz, 2, heads, dh, dh), F32),
                   jax.ShapeDtypeStruct((bsz, 2, heads, dh), F32),
                   jax.ShapeDtypeStruct((bsz, 2, heads, dh), F32)],
        scratch_shapes=[pltpu.VMEM((bt, heads, dh, dh), F32), pltpu.VMEM((bt, heads, dh), F32),
                        pltpu.VMEM((bt, heads, dh), F32)],
        compiler_params=_params("parallel", "parallel", "arbitrary"),
        name="mlstm_scan",
    )(qk, qk, u_ml, g2, gt2, jnp.swapaxes(c0, -1, -2), n0, m0)
    return h, jnp.swapaxes(ct, -1, -2), n, m


def _mix_out_kernel(yhy_ref, hf_ref, hb_ref, o_ref, x_ref, mod_ref, mlw_ref, wout_ref, fnw_ref,
                    x1_ref, h2_ref, *, heads, dh):
    m = mod_ref[0]
    h = hf_ref[0, 0] + hb_ref[0, 0]
    o = o_ref[0]
    mlw = mlw_ref[...]
    parts = [yhy_ref[0].astype(BF16)]
    for hd in range(heads):
        sl = slice(hd * dh, (hd + 1) * dh)
        parts.append((_rms(h[:, sl], mlw[:, sl]) * jax.nn.sigmoid(o[:, sl])).astype(BF16))
    y = jnp.concatenate(parts, axis=-1)
    x1 = x_ref[0] + m[2:3] * jnp.dot(y, wout_ref[...], preferred_element_type=F32)
    x1_ref[0] = x1
    h2_ref[0] = (_rms(x1, fnw_ref[...]) * (1.0 + m[4:5]) + m[3:4]).astype(BF16)


def _mix_out(y_hy, h2dir, u_ml, x, mod, mod_row, ml_norm_w, w_out, norm_ffn_w):
    bsz, seq, d = x.shape
    d_ml = h2dir.shape[-1]
    tb = 256
    tok = lambda n, j=0: pl.BlockSpec((1, tb, n), lambda b, i: (b, i, j))
    hdir = lambda dr: pl.BlockSpec((1, 1, tb, d_ml), lambda b, i: (dr, b, i, 0))
    full = lambda a: pl.BlockSpec(a.shape, lambda b, i: (0,) * a.ndim)
    return pl.pallas_call(
        functools.partial(_mix_out_kernel, heads=ML_HEADS, dh=d_ml // ML_HEADS),
        grid=(bsz, seq // tb),
        in_specs=[tok(y_hy.shape[-1]), hdir(0), hdir(1), tok(d_ml, 3), tok(d),
                  pl.BlockSpec((1, 6, d), lambda b, i: (mod_row(b), 0, 0)),
                  full(ml_norm_w), full(w_out), full(norm_ffn_w)],
        out_specs=[tok(d), tok(d)],
        out_shape=[jax.ShapeDtypeStruct((bsz, seq, d), F32),
                   jax.ShapeDtypeStruct((bsz, seq, d), BF16)],
        compiler_params=_params("parallel", "parallel"),
        name="mixer_out",
    )(y_hy, h2dir, h2dir, u_ml, x, mod, ml_norm_w, w_out, norm_ffn_w)


def _ffn_kernel(*refs, width, halo, final):
    if halo:
        (top_ref, main_ref, bot_ref, x1_ref, mod_ref, wa_ref, wv_ref, cwa_ref, cwv_ref,
         cba_ref, cbv_ref, wd_ref, fw_ref, o_ref, acc_ref) = refs
    else:
        (main_ref, x1_ref, mod_ref, wa_ref, wv_ref, cwa_ref, cwv_ref,
         cba_ref, cbv_ref, wd_ref, fw_ref, o_ref, acc_ref) = refs
    j = pl.program_id(2)
    d = main_ref.shape[-1]
    hm = main_ref[...].reshape(-1, d)
    m_rows = hm.shape[0]

    @pl.when(j == 0)
    def _():
        acc_ref[...] = jnp.zeros_like(acc_ref)

    col = lax.broadcasted_iota(jnp.int32, (m_rows, 1), 0) % width
    not_first = (col != 0).astype(F32)
    not_last = (col != width - 1).astype(F32)

    def conv(w_ref, cw_ref, cb_ref):
        w = w_ref[...]
        cw = cw_ref[...]
        um = jnp.dot(hm, w, preferred_element_type=F32)
        slabs = [(1, um)]
        if halo:
            r = pl.program_id(1)
            top_ok = jnp.where(r > 0, 1.0, 0.0)
            bot_ok = jnp.where(r < pl.num_programs(1) - 1, 1.0, 0.0)
            ut = jnp.dot(top_ref[0], w, preferred_element_type=F32) * top_ok
            ub = jnp.dot(bot_ref[0], w, preferred_element_type=F32) * bot_ok
            full = jnp.concatenate([ut, um, ub], axis=0)
            slabs = [(kr, full[kr * width:kr * width + m_rows]) for kr in range(3)]
        left = sum(s * cw[3 * kr:3 * kr + 1] for kr, s in slabs)
        mid = sum(s * cw[3 * kr + 1:3 * kr + 2] for kr, s in slabs)
        right = sum(s * cw[3 * kr + 2:3 * kr + 3] for kr, s in slabs)
        return (mid + not_first * pltpu.roll(left, 1, 0)
                + not_last * pltpu.roll(right, m_rows - 1, 0) + cb_ref[...])

    a = conv(wa_ref, cwa_ref, cba_ref)
    val = conv(wv_ref, cwv_ref, cbv_ref)
    act = (_silu(a) * val).astype(BF16)
    acc_ref[...] += jnp.dot(act, wd_ref[...], preferred_element_type=F32)

    @pl.when(j == pl.num_programs(2) - 1)
    def _():
        m = mod_ref[0]
        x2 = x1_ref[...].reshape(-1, d) + m[5:6] * acc_ref[...]
        if final:
            x2 = _rms(x2, fw_ref[...])
        o_ref[...] = x2.reshape(o_ref.shape)


def _ffn(h2, x1, mod, mod_row, w_up, conv_w, conv_b, w_down, final_w, rows, final):
    bsz, seq, d = x1.shape
    d_ff = w_down.shape[0]
    width = seq // rows
    tn = 256
    nj = d_ff // tn
    halo = rows > 1
    if halo:
        rb = 16
        nb, m_tok, nr = 1, rb * width, rows // rb
    else:
        nb, m_tok, nr = math.gcd(bsz, 4), seq, 1
    tok = pl.BlockSpec((nb, m_tok, d), lambda b, r, j: (b, r, 0))
    wcol = lambda k, n0: pl.BlockSpec((k, tn), lambda b, r, j: (0, n0 + j))
    in_specs, args = [], []
    if halo:
        in_specs += [pl.BlockSpec((1, width, d), lambda b, r, j: (b, jnp.maximum(r * rb - 1, 0), 0)),
                     tok,
                     pl.BlockSpec((1, width, d), lambda b, r, j: (b, jnp.minimum((r + 1) * rb, rows - 1), 0))]
        args += [h2, h2, h2]
    else:
        in_specs += [tok]
        args += [h2]
    in_specs += [tok, pl.BlockSpec((1, 6, d), lambda b, r, j: (mod_row(b), 0, 0)),
                 wcol(d, 0), wcol(d, nj), wcol(9, 0), wcol(9, nj), wcol(1, 0), wcol(1, nj),
                 pl.BlockSpec((tn, d), lambda b, r, j: (j, 0)),
                 pl.BlockSpec((1, d), lambda b, r, j: (0, 0))]
    args += [x1, mod, w_up, w_up, conv_w, conv_w, conv_b, conv_b, w_down, final_w]
    return pl.pallas_call(
        functools.partial(_ffn_kernel, width=width, halo=halo, final=final),
        grid=(bsz // nb, nr, nj),
        in_specs=in_specs,
        out_specs=tok,
        out_shape=jax.ShapeDtypeStruct((bsz, seq, d), F32),
        scratch_shapes=[pltpu.VMEM((nb * m_tok, d), F32)],
        compiler_params=_params("parallel", "parallel", "arbitrary"),
        name="conv_ffn",
    )(*args)


def _layer(x, mod, mod_row, p, tabs, c0, n0, m0, rows, final):
    bsz, seq, d = x.shape
    d_hy, d_ml = p["d_hy"], p["d_ml"]
    heads = ML_HEADS
    ctab, stab, sttab = tabs
    u_hy, u_ml, g = _inproj(x, mod, mod_row, p["norm_mix_w"], p["w_hy"], p["w_ml"], p["w_g"], p["b_g"])
    v, x0 = _hy_pre(u_hy, p["hy_conv_w"], p["hy_conv_b"])
    a, dd, ny = _filter_taps(seq, p["hy_filt"], d_hy)
    kre, kim = _filter_dft(ctab, stab, a, dd, ny)
    bt = math.gcd(bsz, 4)
    yre, yim = _dft_fwd(ctab, stab, v, kre, kim, bt)
    y_hy = _dft_inv(ctab, sttab, yre, yim, v, x0, p["hy_bias"], bt)
    qk = _ml_pre(u_ml, p["ml_conv_w"], p["ml_conv_b"], d_ml, d_ml // heads)
    m0b = jnp.broadcast_to(m0[..., None], m0.shape + (d_ml // heads,))
    h2dir, c_new, n_new, m_new = _mlstm(qk, u_ml, g, c0, n0, m0b, d_ml)
    x1, h2 = _mix_out(y_hy, h2dir, u_ml, x, mod, mod_row, p["ml_norm_w"], p["w_out"], p["norm_ffn_w"])
    x2 = _ffn(h2, x1, mod, mod_row, p["ffn_w_up"], p["ffn_conv_w"], p["ffn_conv_b"],
              p["ffn_w_down"], p["final_norm_w"], rows, final)
    return x2, c_new, n_new, m_new[..., 0]


def _layer_params(l, d_hy, d_ml, norm_mix_w, w_in, b_gate, hy_conv_w, hy_conv_b, hy_filt, hy_bias,
                  ml_conv_w, ml_conv_b, ml_norm_w, w_out, norm_ffn_w, ffn_w_up, ffn_conv_w,
                  ffn_conv_b, ffn_w_down, final_norm_w):
    w = w_in[l]
    n_gates = b_gate.shape[-1]
    d_ff = ffn_w_down.shape[1]
    wg = jnp.pad(w[:, 3 * d_hy + 4 * d_ml:], ((0, 0), (0, LANE - n_gates)))
    return {
        "d_hy": d_hy, "d_ml": d_ml,
        "norm_mix_w": norm_mix_w[l][None, :],
        "w_hy": w[:, :3 * d_hy].astype(BF16),
        "w_ml": w[:, 3 * d_hy:3 * d_hy + 4 * d_ml].astype(BF16),
        "w_g": jnp.concatenate(_split_hi_lo(wg), axis=1),
        "b_g": jnp.pad(b_gate[l], (0, LANE - n_gates))[None, :],
        "hy_conv_w": hy_conv_w[l], "hy_conv_b": hy_conv_b[l][None, :],
        "hy_filt": tuple(a[l] for a in hy_filt),
        "hy_bias": hy_bias[l][None, :],
        "ml_conv_w": ml_conv_w[l], "ml_conv_b": ml_conv_b[l][None, :],
        "ml_norm_w": ml_norm_w[l][None, :],
        "w_out": w_out[l].astype(BF16),
        "norm_ffn_w": norm_ffn_w[l][None, :],
        "ffn_w_up": ffn_w_up[l].astype(BF16),
        "ffn_conv_w": ffn_conv_w[l].reshape(9, 2 * d_ff),
        "ffn_conv_b": ffn_conv_b[l][None, :],
        "ffn_w_down": ffn_w_down[l].astype(BF16),
        "final_norm_w": final_norm_w[None, :],
    }


def kernel(x_prompt, x_sample, state_mlstm_C, state_mlstm_n, state_mlstm_m, c, c_ctx, ada_w, ada_b, norm_mix_w, w_in, b_gate, hy_conv_w, hy_conv_b, hy_filt_w1, hy_filt_b1, hy_filt_w2, hy_filt_b2, hy_filt_w3, hy_filt_b3, hy_filt_w4, hy_freq, hy_bias, ml_conv_w, ml_conv_b, ml_norm_w, w_out, norm_ffn_w, ffn_w_up, ffn_conv_w, ffn_conv_b, ffn_w_down, final_norm_w):
    depth, d, _ = ada_w.shape
    bp, seq_p, _ = x_prompt.shape
    bs, seq_s, _ = x_sample.shape
    heads = ML_HEADS
    d_hy = hy_bias.shape[-1]
    d_ml = ml_norm_w.shape[-1]
    dh = d_ml // heads
    assert bs < MOD_ROWS and seq_s % GRID_W == 0

    cvec = jnp.concatenate([c, c_ctx[None, :], jnp.zeros((MOD_ROWS - bs - 1, d), F32)], axis=0)
    mod_all = _ada(cvec, ada_w, ada_b).reshape(depth, MOD_ROWS, 6, d)
    row_ctx = lambda b: bs
    row_lat = lambda b: b

    tabs_p = _dft_tables(seq_p)
    tabs_s = _dft_tables(seq_s)
    zc = jnp.zeros((bp, 2, heads, dh, dh), F32)
    zn = jnp.zeros((bp, 2, heads, dh), F32)
    zm = jnp.zeros((bp, 2, heads), F32)
    hy_filt = (hy_filt_w1, hy_filt_b1, hy_filt_w2, hy_filt_b2, hy_filt_w3, hy_filt_b3, hy_filt_w4, hy_freq)

    xp, xs = x_prompt, x_sample
    new_c, new_n, new_m = [], [], []
    for l in range(depth):
        p = _layer_params(l, d_hy, d_ml, norm_mix_w, w_in, b_gate, hy_conv_w, hy_conv_b, hy_filt,
                          hy_bias, ml_conv_w, ml_conv_b, ml_norm_w, w_out, norm_ffn_w, ffn_w_up,
                          ffn_conv_w, ffn_conv_b, ffn_w_down, final_norm_w)
        final = l == depth - 1
        xp, cl, nl, ml = _layer(xp, mod_all[l], row_ctx, p, tabs_p, zc, zn, zm, 1, final)
        new_c.append(cl)
        new_n.append(nl)
        new_m.append(ml)
        xs, _, _, _ = _layer(xs, mod_all[l], row_lat, p, tabs_s, state_mlstm_C[:, l],
                             state_mlstm_n[:, l], state_mlstm_m[:, l], seq_s // GRID_W, final)
    return (xp, xs, jnp.stack(new_c, axis=1), jnp.stack(new_n, axis=1), jnp.stack(new_m, axis=1))
```

```python
import functools
import math

import numpy as np
import jax
import jax.numpy as jnp
from jax import lax
from jax.experimental import pallas as pl
from jax.experimental.pallas import tpu as pltpu

F32 = jnp.float32
BF16 = jnp.bfloat16
HIGHEST = lax.Precision.HIGHEST

GRID_W = 64
ML_HEADS = 4
CHUNK = 128
HY_BANDS = 16
HY_DECAY_TARGET = 1e-2
HY_FAST_DECAY_PCT = 0.3
HY_SLOW_DECAY_PCT = 1.5
HY_MIN_DECAY = math.log(HY_DECAY_TARGET) / HY_SLOW_DECAY_PCT
HY_MAX_DECAY = math.log(HY_DECAY_TARGET) / HY_FAST_DECAY_PCT
EPS = 1e-6

LANE = 128
HALO_ROWS = 16
VMEM_LIMIT = 56 * 1024 * 1024
MOD_ROWS = 8
DFT_ROW_TILE = 64


def _params(*sem):
    return pltpu.CompilerParams(dimension_semantics=sem, vmem_limit_bytes=VMEM_LIMIT)


def _silu(x):
    return x * jax.nn.sigmoid(x)


def _rms(x, w):
    return x * lax.rsqrt(jnp.mean(x * x, axis=-1, keepdims=True) + EPS) * w


def _split_hi_lo(x):
    hi = x.astype(BF16)
    return hi, (x - hi.astype(F32)).astype(BF16)


def _ada_kernel(c_ref, w_ref, b_ref, o_ref):
    s = _silu(c_ref[...])
    o_ref[0] = jnp.dot(s, w_ref[0], precision=HIGHEST, preferred_element_type=F32) + b_ref[0]


def _ada(cvec, ada_w, ada_b):
    depth, d, nmod = ada_w.shape
    tn = 1024
    return pl.pallas_call(
        _ada_kernel,
        grid=(depth, nmod // tn),
        in_specs=[
            pl.BlockSpec((MOD_ROWS, d), lambda l, j: (0, 0)),
            pl.BlockSpec((1, d, tn), lambda l, j: (l, 0, j)),
            pl.BlockSpec((1, 1, tn), lambda l, j: (l, 0, j)),
        ],
        out_specs=pl.BlockSpec((1, MOD_ROWS, tn), lambda l, j: (l, 0, j)),
        out_shape=jax.ShapeDtypeStruct((depth, MOD_ROWS, nmod), F32),
        compiler_params=_params("parallel", "parallel"),
        name="ada_mod",
    )(cvec, ada_w, ada_b.reshape(depth, 1, nmod))


def _inproj_kernel(x_ref, mod_ref, nw_ref, why_ref, wml_ref, wg_ref, bg_ref,
                   uhy_ref, uml_ref, g_ref):
    m = mod_ref[0]
    h = _rms(x_ref[0], nw_ref[...]) * (1.0 + m[1:2]) + m[0:1]
    h_hi, h_lo = _split_hi_lo(h)
    uhy_ref[0] = jnp.dot(h_hi, why_ref[...], preferred_element_type=F32).astype(uhy_ref.dtype)
    uml_ref[0] = jnp.dot(h_hi, wml_ref[...], preferred_element_type=F32).astype(uml_ref.dtype)
    wg = wg_ref[...]
    ng = g_ref.shape[-1]
    a = jnp.dot(h_hi, wg, preferred_element_type=F32)
    b = jnp.dot(h_lo, wg[:, :ng], preferred_element_type=F32)
    g_ref[0] = a[:, :ng] + a[:, ng:] + b + bg_ref[...]


def _inproj(x, mod, mod_row, nw, w_hy, w_ml, w_g, b_g):
    bsz, seq, d = x.shape
    tb = min(seq, 512)
    n_hy, n_ml = w_hy.shape[1], w_ml.shape[1]
    tok = lambda n: pl.BlockSpec((1, tb, n), lambda b, i: (b, i, 0))
    full = lambda a: pl.BlockSpec(a.shape, lambda b, i: (0,) * a.ndim)
    return pl.pallas_call(
        _inproj_kernel,
        grid=(bsz, seq // tb),
        in_specs=[tok(d), pl.BlockSpec((1, 6, d), lambda b, i: (mod_row(b), 0, 0)),
                  full(nw), full(w_hy), full(w_ml), full(w_g), full(b_g)],
        out_specs=[tok(n_hy), tok(n_ml), tok(LANE)],
        out_shape=[jax.ShapeDtypeStruct((bsz, seq, n_hy), BF16),
                   jax.ShapeDtypeStruct((bsz, seq, n_ml), BF16),
                   jax.ShapeDtypeStruct((bsz, seq, LANE), F32)],
        compiler_params=_params("parallel", "parallel"),
        name="inproj",
    )(x, mod, nw, w_hy, w_ml, w_g, b_g)


def _dwconv3(u, before, after, w, b):
    n = u.shape[0]
    row = lax.broadcasted_iota(jnp.int32, u.shape, 0)
    prev = jnp.where(row == 0, before, pltpu.roll(u, 1, 0))
    nxt = jnp.where(row == n - 1, after, pltpu.roll(u, n - 1, 0))
    return prev * w[0:1] + u * w[1:2] + nxt * w[2:3] + b


def _tile_neighbours(prev_ref, next_ref):
    i = pl.program_id(1)
    before = jnp.where(i == 0, 0.0, prev_ref[0].astype(F32)[HALO_ROWS - 1:HALO_ROWS])
    after = jnp.where(i == pl.num_programs(1) - 1, 0.0, next_ref[0].astype(F32)[0:1])
    return before, after


def _row_tile_specs(seq, tb, c, cblk=0):
    per = tb // HALO_ROWS
    last = seq // HALO_ROWS - 1
    main = pl.BlockSpec((1, tb, c), lambda b, i: (b, i, cblk))
    prev = pl.BlockSpec((1, HALO_ROWS, c), lambda b, i: (b, jnp.maximum(i * per - 1, 0), cblk))
    nxt = pl.BlockSpec((1, HALO_ROWS, c), lambda b, i: (b, jnp.minimum((i + 1) * per, last), cblk))
    return main, prev, nxt


def _hy_pre_kernel(u_ref, up_ref, un_ref, w_ref, b_ref, v_ref, x0_ref):
    before, after = _tile_neighbours(up_ref, un_ref)
    y = _dwconv3(u_ref[0].astype(F32), before, after, w_ref[...], b_ref[...])
    c = v_ref.shape[-1]
    x0_ref[0] = y[:, :c].astype(x0_ref.dtype)
    v_ref[0] = (y[:, 2 * c:] * y[:, c:2 * c]).astype(v_ref.dtype)


def _hy_pre(u_hy, conv_w, conv_b):
    bsz, seq, c3 = u_hy.shape
    tb = min(seq, 512)
    main, prev, nxt = _row_tile_specs(seq, tb, c3)
    full = lambda a: pl.BlockSpec(a.shape, lambda b, i: (0,) * a.ndim)
    out = pl.BlockSpec((1, tb, c3 // 3), lambda b, i: (b, i, 0))
    shp = jax.ShapeDtypeStruct((bsz, seq, c3 // 3), BF16)
    return pl.pallas_call(
        _hy_pre_kernel,
        grid=(bsz, seq // tb),
        in_specs=[main, prev, nxt, full(conv_w), full(conv_b)],
        out_specs=[out, out],
        out_shape=[shp, shp],
        compiler_params=_params("parallel", "parallel"),
        name="hyena_pre",
    )(u_hy, u_hy, u_hy, conv_w, conv_b)


def _ml_pre_kernel(u_ref, up_ref, un_ref, w_ref, b_ref, o_ref, *, d_ml, kscale):
    before, after = _tile_neighbours(up_ref, un_ref)
    y = _silu(_dwconv3(u_ref[0].astype(F32), before, after, w_ref[...], b_ref[...]))
    lane = lax.broadcasted_iota(jnp.int32, (1, y.shape[1]), 1)
    o_ref[0] = (y * jnp.where(lane >= d_ml, kscale, 1.0)).astype(o_ref.dtype)


def _ml_pre(u_ml, conv_w, conv_b, d_ml, dh):
    bsz, seq, _ = u_ml.shape
    tb = min(seq, 512)
    main, prev, nxt = _row_tile_specs(seq, tb, 2 * d_ml)
    full = lambda a: pl.BlockSpec(a.shape, lambda b, i: (0,) * a.ndim)
    return pl.pallas_call(
        functools.partial(_ml_pre_kernel, d_ml=d_ml, kscale=dh ** -0.5),
        grid=(bsz, seq // tb),
        in_specs=[main, prev, nxt, full(conv_w), full(conv_b)],
        out_specs=pl.BlockSpec((1, tb, 2 * d_ml), lambda b, i: (b, i, 0)),
        out_shape=jax.ShapeDtypeStruct((bsz, seq, 2 * d_ml), BF16),
        compiler_params=_params("parallel", "parallel"),
        name="mlstm_pre",
    )(u_ml, u_ml, u_ml, conv_w, conv_b)


def _dft_table_kernel(pre_ref, pim_ref, qre_ref, qim_ref, c_ref, s_ref, st_ref):
    pre, pim = pre_ref[0], pim_ref[0]
    qre, qim = qre_ref[...], qim_ref[...]
    c = pre * qre - pim * qim
    s = pre * qim + pim * qre
    grow = lax.broadcasted_iota(jnp.int32, c.shape, 0) + pl.program_id(0) * c.shape[0]
    lane = lax.broadcasted_iota(jnp.int32, c.shape, 1)
    alt_lane = (1 - 2 * (lane & 1)).astype(F32)
    alt_row = (1 - 2 * (grow & 1)).astype(F32)
    c_ref[...] = c.astype(BF16)
    s_ref[...] = jnp.where(grow == 0, alt_lane, s).astype(BF16)
    st_ref[...] = jnp.where(lane == 0, alt_row, s).astype(BF16)


def _dft_tables(seq):
    n = 2 * seq
    tm = DFT_ROW_TILE
    s = jnp.arange(seq, dtype=jnp.int32)[None, :]
    ang = lambda f: (-2.0 * math.pi / n) * ((f * s) % n).astype(F32)
    fp = (jnp.arange(seq // tm, dtype=jnp.int32) * tm)[:, None]
    fq = jnp.arange(tm, dtype=jnp.int32)[:, None]
    ap, aq = ang(fp), ang(fq)
    p3 = lambda a: a.reshape(seq // tm, 1, seq)
    pspec = pl.BlockSpec((1, 1, seq), lambda r: (r, 0, 0))
    qspec = pl.BlockSpec((tm, seq), lambda r: (0, 0))
    out = pl.BlockSpec((tm, seq), lambda r: (r, 0))
    shp = jax.ShapeDtypeStruct((seq, seq), BF16)
    return pl.pallas_call(
        _dft_table_kernel,
        grid=(seq // tm,),
        in_specs=[pspec, pspec, qspec, qspec],
        out_specs=[out, out, out],
        out_shape=[shp, shp, shp],
        compiler_params=_params("parallel"),
        name="dft_tables",
    )(p3(jnp.cos(ap)), p3(jnp.sin(ap)), jnp.cos(aq), jnp.sin(aq))


def _filter_kernel(z_ref, w1_ref, b1_ref, w2_ref, b2_ref, w3_ref, b3_ref, fr_ref,
                   w4f_ref, w4b_ref, dl_ref, a_ref, d_ref, ny_ref, h_ref):
    dot = functools.partial(jnp.dot, precision=HIGHEST, preferred_element_type=F32)

    @pl.when(pl.program_id(0) == 0)
    def _():
        fr = fr_ref[...]
        h = jnp.sin(fr * (dot(z_ref[...], w1_ref[...]) + b1_ref[...]))
        h = jnp.sin(fr * (dot(h, w2_ref[...]) + b2_ref[...]))
        h_ref[...] = jnp.sin(fr * (dot(h, w3_ref[...]) + b3_ref[...]))

    h = h_ref[...]
    decay = jnp.exp(-z_ref[:, 0:1] * dl_ref[...])
    hf = dot(h, w4f_ref[...]) * decay
    hb = dot(h, w4b_ref[...]) * decay
    row = lax.broadcasted_iota(jnp.int32, hf.shape, 0)
    hb = jnp.where(row == 0, 0.0, hb)
    inv = 1.0 / jnp.sum(jnp.abs(hf) + jnp.abs(hb), axis=0, keepdims=True)
    a = (hf + hb) * inv
    a_ref[...] = a.astype(BF16)
    d_ref[...] = ((hf - hb) * inv).astype(BF16)
    ny_ref[...] = jnp.sum(a * (1 - 2 * (row & 1)).astype(F32), axis=0, keepdims=True)


def _filter_taps(seq, filt, d_hy):
    w1, b1, w2, b2, w3, b3, w4, freq = filt
    emb, ffn = w1.shape
    t = jnp.linspace(0.0, 1.0, seq, dtype=F32)[:, None]
    wpos = (2.0 * math.pi / seq) * jnp.arange(seq, dtype=F32)[:, None]
    bands = jnp.linspace(1e-4, HY_BANDS - 1, HY_BANDS, dtype=F32)[None, :]
    z = jnp.concatenate([t, jnp.cos(bands * wpos), -jnp.sin(bands * wpos),
                         jnp.zeros((seq, LANE - emb), F32)], axis=-1)
    w1p = jnp.concatenate([w1, jnp.zeros((LANE - emb, ffn), F32)], axis=0)
    deltas = jnp.abs(jnp.linspace(HY_MIN_DECAY, HY_MAX_DECAY, d_hy, dtype=F32))[None, :]
    row = lambda a: a.reshape(1, -1)
    nblk = d_hy // LANE
    full = lambda a: pl.BlockSpec(a.shape, lambda j: (0,) * a.ndim)
    args = [z, w1p, row(b1), w2, row(b2), w3, row(b3), row(freq)]
    col = pl.BlockSpec((seq, LANE), lambda j: (0, j))
    vec = pl.BlockSpec((1, LANE), lambda j: (0, j))
    return pl.pallas_call(
        _filter_kernel,
        grid=(nblk,),
        in_specs=[full(a) for a in args] + [
            pl.BlockSpec((ffn, LANE), lambda j: (0, j)),
            pl.BlockSpec((ffn, LANE), lambda j: (0, nblk + j)), vec],
        out_specs=[col, col, vec],
        out_shape=[jax.ShapeDtypeStruct((seq, d_hy), BF16),
                   jax.ShapeDtypeStruct((seq, d_hy), BF16),
                   jax.ShapeDtypeStruct((1, d_hy), F32)],
        scratch_shapes=[pltpu.VMEM((seq, ffn), F32)],
        compiler_params=_params("arbitrary"),
        name="hyena_filter",
    )(*args, w4, w4, deltas)


def _filter_dft_kernel(c_ref, s_ref, a_ref, d_ref, ny_ref, kre_ref, kim_ref, are_ref, aim_ref):
    k = pl.program_id(1)

    @pl.when(k == 0)
    def _():
        are_ref[...] = jnp.zeros_like(are_ref)
        aim_ref[...] = jnp.zeros_like(aim_ref)

    are_ref[...] += jnp.dot(c_ref[...], a_ref[...], preferred_element_type=F32)
    aim_ref[...] += jnp.dot(s_ref[...], d_ref[...], preferred_element_type=F32)

    @pl.when(k == pl.num_programs(1) - 1)
    def _():
        grow = lax.broadcasted_iota(jnp.int32, are_ref.shape, 0) + pl.program_id(0) * are_ref.shape[0]
        kre_ref[...] = are_ref[...]
        kim_ref[...] = jnp.where(grow == 0, ny_ref[...], aim_ref[...])


def _filter_dft(ctab, stab, a, d, ny):
    seq, c = a.shape
    t = min(seq, 512)
    return pl.pallas_call(
        _filter_dft_kernel,
        grid=(seq // t, seq // t),
        in_specs=[pl.BlockSpec((t, t), lambda m, k: (m, k)),
                  pl.BlockSpec((t, t), lambda m, k: (m, k)),
                  pl.BlockSpec((t, c), lambda m, k: (k, 0)),
                  pl.BlockSpec((t, c), lambda m, k: (k, 0)),
                  pl.BlockSpec((1, c), lambda m, k: (0, 0))],
        out_specs=[pl.BlockSpec((t, c), lambda m, k: (m, 0))] * 2,
        out_shape=[jax.ShapeDtypeStruct((seq, c), F32)] * 2,
        scratch_shapes=[pltpu.VMEM((t, c), F32), pltpu.VMEM((t, c), F32)],
        compiler_params=_params("parallel", "arbitrary"),
        name="hyena_filter_dft",
    )(ctab, stab, a, d, ny)


def _dft_fwd_kernel(c_ref, s_ref, v_ref, kre_ref, kim_ref, yre_ref, yim_ref, are_ref, aim_ref):
    k = pl.program_id(2)
    bt = v_ref.shape[0]

    @pl.when(k == 0)
    def _():
        are_ref[...] = jnp.zeros_like(are_ref)
        aim_ref[...] = jnp.zeros_like(aim_ref)

    for b in range(bt):
        vb = v_ref[b].astype(BF16)
        are_ref[b] += jnp.dot(c_ref[...], vb, preferred_element_type=F32)
        aim_ref[b] += jnp.dot(s_ref[...], vb, preferred_element_type=F32)

    @pl.when(k == pl.num_programs(2) - 1)
    def _():
        kre, kim = kre_ref[...], kim_ref[...]
        grow = lax.broadcasted_iota(jnp.int32, kre.shape, 0) + pl.program_id(1) * kre.shape[0]
        packed = grow == 0
        for b in range(bt):
            xr, xi = are_ref[b], aim_ref[b]
            yre = jnp.where(packed, 0.5 * xr * kre, xr * kre - xi * kim)
            yim = jnp.where(packed, 0.5 * xi * kim, xr * kim + xi * kre)
            yre_ref[b] = yre.astype(BF16)
            yim_ref[b] = yim.astype(BF16)


def _dft_fwd(ctab, stab, v, kre, kim, bt):
    bsz, seq, c = v.shape
    t = min(seq, 512)
    tab = pl.BlockSpec((t, t), lambda b, m, k: (m, k))
    kspec = pl.BlockSpec((t, c), lambda b, m, k: (m, 0))
    out = pl.BlockSpec((bt, t, c), lambda b, m, k: (b, m, 0))
    shp = jax.ShapeDtypeStruct((bsz, seq, c), BF16)
    return pl.pallas_call(
        _dft_fwd_kernel,
        grid=(bsz // bt, seq // t, seq // t),
        in_specs=[tab, tab, pl.BlockSpec((bt, t, c), lambda b, m, k: (b, k, 0)), kspec, kspec],
        out_specs=[out, out],
        out_shape=[shp, shp],
        scratch_shapes=[pltpu.VMEM((bt, t, c), F32), pltpu.VMEM((bt, t, c), F32)],
        compiler_params=_params("parallel", "parallel", "arbitrary"),
        name="hyena_dft_fwd",
    )(ctab, stab, v, kre, kim)


def _dft_inv_kernel(c_ref, st_ref, yre_ref, yim_ref, v_ref, x0_ref, hb_ref, o_ref, acc_ref, *, scale):
    k = pl.program_id(2)
    bt = v_ref.shape[0]

    @pl.when(k == 0)
    def _():
        acc_ref[...] = jnp.zeros_like(acc_ref)

    for b in range(bt):
        acc_ref[b] += (jnp.dot(c_ref[...], yre_ref[b], preferred_element_type=F32)
                       + jnp.dot(st_ref[...], yim_ref[b], preferred_element_type=F32))

    @pl.when(k == pl.num_programs(2) - 1)
    def _():
        for b in range(bt):
            v = v_ref[b]
            y = (acc_ref[b] * scale + hb_ref[...] * v.astype(F32)) * x0_ref[b].astype(F32)
            o_ref[b] = y.astype(o_ref.dtype)


def _dft_inv(ctab, sttab, yre, yim, v, x0, hy_bias, bt):
    bsz, seq, c = v.shape
    t = min(seq, 512)
    tab = pl.BlockSpec((t, t), lambda b, m, k: (m, k))
    yspec = pl.BlockSpec((bt, t, c), lambda b, m, k: (b, k, 0))
    tok = pl.BlockSpec((bt, t, c), lambda b, m, k: (b, m, 0))
    return pl.pallas_call(
        functools.partial(_dft_inv_kernel, scale=1.0 / seq),
        grid=(bsz // bt, seq // t, seq // t),
        in_specs=[tab, tab, yspec, yspec, tok, tok, pl.BlockSpec((1, c), lambda b, m, k: (0, 0))],
        out_specs=tok,
        out_shape=jax.ShapeDtypeStruct((bsz, seq, c), BF16),
        scratch_shapes=[pltpu.VMEM((bt, t, c), F32)],
        compiler_params=_params("parallel", "parallel", "arbitrary"),
        name="hyena_dft_inv",
    )(ctab, sttab, yre, yim, v, x0, hy_bias)


def _log_sigmoid(x):
    return jnp.minimum(x, 0.0) - jnp.log1p(jnp.exp(-jnp.abs(x)))


def _mlstm_kernel(q_ref, k_ref, v_ref, g_ref, gt_ref, c0_ref, n0_ref, m0_ref,
                  h_ref, c_out_ref, n_out_ref, m_out_ref, c_s, n_s, m_s, *, heads, dh):
    step = pl.program_id(2)
    fwd = pl.program_id(0) == 0
    bt, t = q_ref.shape[0], q_ref.shape[1]

    @pl.when(step == 0)
    def _():
        c_s[...] = c0_ref[:, 0]
        n_s[...] = n0_ref[:, 0]
        m_s[...] = m0_ref[:, 0]

    row = lax.broadcasted_iota(jnp.int32, (t, t), 0)
    col = lax.broadcasted_iota(jnp.int32, (t, t), 1)
    sgn = jnp.where(fwd, 1, -1)
    seen = sgn * (row - col) >= 0
    seen_b = seen.astype(BF16)
    seen_tb = (sgn * (col - row) >= 0).astype(BF16)
    nt = (((1,), (1,)), ((), ()))
    tn = (((0,), (0,)), ((), ()))

    def split3(x, axis):
        hi = x.astype(BF16).astype(F32)
        r = x - hi
        mid = r.astype(BF16).astype(F32)
        return jnp.concatenate([hi, mid, r - mid], axis=axis).astype(BF16)

    g = g_ref[0, 0]
    gt = gt_ref[0, 0]
    lf = _log_sigmoid(g)
    lf_tot = jnp.sum(lf, axis=0, keepdims=True)
    cs = jnp.dot(seen_b, split3(lf, 1), preferred_element_type=F32)
    nl = lf.shape[1]
    bcol_all = cs[:, :nl] + cs[:, nl:2 * nl] + cs[:, 2 * nl:]
    rs = jnp.dot(split3(_log_sigmoid(gt), 0), seen_tb, preferred_element_type=F32)
    nr = gt.shape[0]
    brow_all = rs[:nr] + rs[nr:2 * nr] + rs[2 * nr:]
    chains = [(b, hd) for b in range(bt) for hd in range(heads)]
    sl = lambda hd: slice(hd * dh, (hd + 1) * dh)
    qs = [q_ref[b, :, sl(hd)] for b, hd in chains]
    ks = [k_ref[b, :, sl(hd)] for b, hd in chains]
    vs = [v_ref[b, :, sl(hd)].astype(BF16) for b, hd in chains]
    qk = [lax.dot_general(q, k, nt, preferred_element_type=F32) for q, k in zip(qs, ks)]
    each = lambda f, *ls: [f(*a) for a in zip(*ls)]
    fcols = [b * 2 * heads + heads + hd for b, hd in chains]
    bc = [bcol_all[:, c:c + 1] for c in fcols]
    rowt = [gt[c - heads:c - heads + 1, :] - brow_all[c:c + 1, :] for c in fcols]
    ic = [g[:, c - heads:c - heads + 1] for c in fcols]
    b_end = [lf_tot[:, c:c + 1] for c in fcols]
    m_prev = [m_s[b, hd:hd + 1, 0:1] for b, hd in chains]
    ct_prev = [c_s[b, hd] for b, hd in chains]
    n_prev = [n_s[b, hd:hd + 1, :] for b, hd in chains]
    dmat = each(lambda x, y: jnp.where(seen, x + y, -jnp.inf), bc, rowt)
    inter = each(lambda x, m: x + m, bc, m_prev)
    dmax = each(lambda d: jnp.max(d, axis=1, keepdims=True), dmat)
    m_t = each(jnp.maximum, inter, dmax)
    s = each(lambda r, d, m: r * jnp.exp(d - m), qk, dmat, m_t)
    w_inter = each(lambda i, m: jnp.exp(i - m), inter, m_t)
    qf = each(lambda q: q.astype(F32), qs)
    lhs = each(lambda x, q, w: jnp.concatenate([x.astype(BF16), (q * w).astype(BF16)], axis=1),
               s, qf, w_inter)
    rhs = each(lambda v, c: jnp.concatenate([v, c.astype(BF16)], axis=0), vs, ct_prev)
    nums = each(lambda l, r: jnp.dot(l, r, preferred_element_type=F32), lhs, rhs)
    qn = each(lambda q, n: jnp.sum(q * n, axis=1, keepdims=True), qf, n_prev)
    ssum = each(lambda x: jnp.sum(x, axis=1, keepdims=True), s)
    den = each(lambda a, w, x: a + w * x, ssum, w_inter, qn)
    w_log = each(lambda e, x, i: e - x + i, b_end, bc, ic)
    wmax = each(lambda w: jnp.max(w, axis=0, keepdims=True), w_log)
    m_new = each(lambda e, m, w: jnp.maximum(e + m, w), b_end, m_prev, wmax)
    w_s = each(lambda w, m: jnp.exp(w - m), w_log, m_new)
    w_c = each(lambda e, m, mn: jnp.exp(e + m - mn), b_end, m_prev, m_new)
    kw = each(lambda k, w: k.astype(F32) * w, ks, w_s)
    upds = each(lambda x, v: lax.dot_general(x.astype(BF16), v, tn, preferred_element_type=F32), kw, vs)
    ksum = each(lambda x: jnp.sum(x, axis=0, keepdims=True), kw)
    for i, (b, hd) in enumerate(chains):
        h_ref[0, b, :, sl(hd)] = nums[i] / jnp.maximum(jnp.abs(den[i]), jnp.exp(-m_t[i]))
        c_s[b, hd] = w_c[i] * ct_prev[i] + upds[i]
        n_s[b, hd:hd + 1, :] = w_c[i] * n_prev[i] + ksum[i]
        m_s[b, hd:hd + 1, :] = jnp.broadcast_to(m_new[i], (1, dh))

    @pl.when(step == pl.num_programs(2) - 1)
    def _():
        c_out_ref[:, 0] = c_s[...]
        n_out_ref[:, 0] = n_s[...]
        m_out_ref[:, 0] = m_s[...]


def _mlstm(qk, u_ml, g, c0, n0, m0, d_ml):
    bsz, seq, _ = qk.shape
    heads = ML_HEADS
    dh = d_ml // heads
    nc = seq // CHUNK
    bt = math.gcd(bsz, 4)
    nbb = bsz // bt
    gi = g[..., :4 * heads].reshape(nbb, bt, seq, 2, 2 * heads)
    g2 = jnp.transpose(gi, (3, 0, 2, 1, 4)).reshape(2, nbb, seq, bt * 2 * heads)
    gt2 = jnp.swapaxes(g2, 2, 3)
    g2 = jnp.pad(g2, ((0, 0), (0, 0), (0, 0), (0, LANE - bt * 2 * heads)))
    chunk = lambda d, c: c + d * (nc - 1 - 2 * c)
    tok = lambda j: pl.BlockSpec((bt, CHUNK, d_ml), lambda d, b, c: (b, chunk(d, c), j))
    st = lambda *tail: pl.BlockSpec((bt, 1, heads) + tail, lambda d, b, c: (b, d, 0) + (0,) * len(tail))
    h, ct, n, m = pl.pallas_call(
        functools.partial(_mlstm_kernel, heads=heads, dh=dh),
        grid=(2, nbb, nc),
        in_specs=[tok(0), tok(1), tok(2),
                  pl.BlockSpec((1, 1, CHUNK, LANE), lambda d, b, c: (d, b, chunk(d, c), 0)),
                  pl.BlockSpec((1, 1, bt * 2 * heads, CHUNK), lambda d, b, c: (d, b, 0, chunk(d, c))),
                  st(dh, dh), st(dh), st(dh)],
        out_specs=[pl.BlockSpec((1, bt, CHUNK, d_ml), lambda d, b, c: (d, b, chunk(d, c), 0)),
                   st(dh, dh), st(dh), st(dh)],
        out_shape=[jax.ShapeDtypeStruct((2, bsz, seq, d_ml), F32),
                   jax.ShapeDtypeStruct((bsz, 2, heads, dh, dh), F32),
                   jax.ShapeDtypeStruct((bsz, 2, heads, dh), F32),
                   jax.ShapeDtypeStruct((bsz, 2, heads, dh), F32)],
        scratch_shapes=[pltpu.VMEM((bt, heads, dh, dh), F32), pltpu.VMEM((bt, heads, dh), F32),
                        pltpu.VMEM((bt, heads, dh), F32)],
        compiler_params=_params("parallel", "parallel", "arbitrary"),
        name="mlstm_scan",
    )(qk, qk, u_ml, g2, gt2, jnp.swapaxes(c0, -1, -2), n0, m0)
    return h, jnp.swapaxes(ct, -1, -2), n, m


def _mix_out_kernel(yhy_ref, hf_ref, hb_ref, o_ref, x_ref, mod_ref, mlw_ref, wout_ref, fnw_ref,
                    x1_ref, h2_ref, *, heads, dh):
    m = mod_ref[0]
    h = hf_ref[0, 0] + hb_ref[0, 0]
    o = o_ref[0].astype(F32)
    mlw = mlw_ref[...]
    parts = [yhy_ref[0].astype(BF16)]
    for hd in range(heads):
        sl = slice(hd * dh, (hd + 1) * dh)
        parts.append((_rms(h[:, sl], mlw[:, sl]) * jax.nn.sigmoid(o[:, sl])).astype(BF16))
    y = jnp.concatenate(parts, axis=-1)
    x1 = x_ref[0] + m[2:3] * jnp.dot(y, wout_ref[...], preferred_element_type=F32)
    x1_ref[0] = x1
    h2_ref[0] = (_rms(x1, fnw_ref[...]) * (1.0 + m[4:5]) + m[3:4]).astype(BF16)


def _mix_out(y_hy, h2dir, u_ml, x, mod, mod_row, ml_norm_w, w_out, norm_ffn_w):
    bsz, seq, d = x.shape
    d_ml = h2dir.shape[-1]
    tb = min(seq, 512)
    tok = lambda n, j=0: pl.BlockSpec((1, tb, n), lambda b, i: (b, i, j))
    hdir = lambda dr: pl.BlockSpec((1, 1, tb, d_ml), lambda b, i: (dr, b, i, 0))
    full = lambda a: pl.BlockSpec(a.shape, lambda b, i: (0,) * a.ndim)
    return pl.pallas_call(
        functools.partial(_mix_out_kernel, heads=ML_HEADS, dh=d_ml // ML_HEADS),
        grid=(bsz, seq // tb),
        in_specs=[tok(y_hy.shape[-1]), hdir(0), hdir(1), tok(d_ml, 3), tok(d),
                  pl.BlockSpec((1, 6, d), lambda b, i: (mod_row(b), 0, 0)),
                  full(ml_norm_w), full(w_out), full(norm_ffn_w)],
        out_specs=[tok(d), tok(d)],
        out_shape=[jax.ShapeDtypeStruct((bsz, seq, d), F32),
                   jax.ShapeDtypeStruct((bsz, seq, d), BF16)],
        compiler_params=_params("parallel", "parallel"),
        name="mixer_out",
    )(y_hy, h2dir, h2dir, u_ml, x, mod, ml_norm_w, w_out, norm_ffn_w)


def _ffn_kernel(*refs, width, halo, final):
    if halo:
        (top_ref, main_ref, bot_ref, x1_ref, mod_ref, wa_ref, wv_ref, cwa_ref, cwv_ref,
         cba_ref, cbv_ref, wd_ref, fw_ref, o_ref, acc_ref) = refs
    else:
        (main_ref, x1_ref, mod_ref, wa_ref, wv_ref, cwa_ref, cwv_ref,
         cba_ref, cbv_ref, wd_ref, fw_ref, o_ref, acc_ref) = refs
    j = pl.program_id(2)
    d = main_ref.shape[-1]
    hm = main_ref[...].reshape(-1, d)
    m_rows = hm.shape[0]

    @pl.when(j == 0)
    def _():
        acc_ref[...] = jnp.zeros_like(acc_ref)

    col = lax.broadcasted_iota(jnp.int32, (m_rows, 1), 0) % width
    not_first = (col != 0).astype(F32)
    not_last = (col != width - 1).astype(F32)

    def conv(w_ref, cw_ref, cb_ref):
        w = w_ref[...]
        cw = cw_ref[...]
        um = jnp.dot(hm, w, preferred_element_type=F32)
        slabs = [(1, um)]
        if halo:
            r = pl.program_id(1)
            top_ok = jnp.where(r > 0, 1.0, 0.0)
            bot_ok = jnp.where(r < pl.num_programs(1) - 1, 1.0, 0.0)
            ut = jnp.dot(top_ref[0], w, preferred_element_type=F32) * top_ok
            ub = jnp.dot(bot_ref[0], w, preferred_element_type=F32) * bot_ok
            full = jnp.concatenate([ut, um, ub], axis=0)
            slabs = [(kr, full[kr * width:kr * width + m_rows]) for kr in range(3)]
        left = sum(s * cw[3 * kr:3 * kr + 1] for kr, s in slabs)
        mid = sum(s * cw[3 * kr + 1:3 * kr + 2] for kr, s in slabs)
        right = sum(s * cw[3 * kr + 2:3 * kr + 3] for kr, s in slabs)
        return (mid + not_first * pltpu.roll(left, 1, 0)
                + not_last * pltpu.roll(right, m_rows - 1, 0) + cb_ref[...])

    a = conv(wa_ref, cwa_ref, cba_ref)
    val = conv(wv_ref, cwv_ref, cbv_ref)
    act = (_silu(a) * val).astype(BF16)
    acc_ref[...] += jnp.dot(act, wd_ref[...], preferred_element_type=F32)

    @pl.when(j == pl.num_programs(2) - 1)
    def _():
        m = mod_ref[0]
        x2 = x1_ref[...].reshape(-1, d) + m[5:6] * acc_ref[...]
        if final:
            x2 = _rms(x2, fw_ref[...])
        o_ref[...] = x2.reshape(o_ref.shape)


def _ffn(h2, x1, mod, mod_row, w_up, conv_w, conv_b, w_down, final_w, rows, final):
    bsz, seq, d = x1.shape
    d_ff = w_down.shape[0]
    width = seq // rows
    tn = 256
    nj = d_ff // tn
    halo = rows > 1
    if halo:
        rb = 16
        nb, m_tok, nr = 1, rb * width, rows // rb
    else:
        nb, m_tok, nr = math.gcd(bsz, 4), seq, 1
    tok = pl.BlockSpec((nb, m_tok, d), lambda b, r, j: (b, r, 0))
    wcol = lambda k, n0: pl.BlockSpec((k, tn), lambda b, r, j: (0, n0 + j))
    in_specs, args = [], []
    if halo:
        in_specs += [pl.BlockSpec((1, width, d), lambda b, r, j: (b, jnp.maximum(r * rb - 1, 0), 0)),
                     tok,
                     pl.BlockSpec((1, width, d), lambda b, r, j: (b, jnp.minimum((r + 1) * rb, rows - 1), 0))]
        args += [h2, h2, h2]
    else:
        in_specs += [tok]
        args += [h2]
    in_specs += [tok, pl.BlockSpec((1, 6, d), lambda b, r, j: (mod_row(b), 0, 0)),
                 wcol(d, 0), wcol(d, nj), wcol(9, 0), wcol(9, nj), wcol(1, 0), wcol(1, nj),
                 pl.BlockSpec((tn, d), lambda b, r, j: (j, 0)),
                 pl.BlockSpec((1, d), lambda b, r, j: (0, 0))]
    args += [x1, mod, w_up, w_up, conv_w, conv_w, conv_b, conv_b, w_down, final_w]
    return pl.pallas_call(
        functools.partial(_ffn_kernel, width=width, halo=halo, final=final),
        grid=(bsz // nb, nr, nj),
        in_specs=in_specs,
        out_specs=tok,
        out_shape=jax.ShapeDtypeStruct((bsz, seq, d), F32),
        scratch_shapes=[pltpu.VMEM((nb * m_tok, d), F32)],
        compiler_params=_params("parallel", "parallel", "arbitrary"),
        name="conv_ffn",
    )(*args)


def _layer(x, mod, mod_row, p, tabs, c0, n0, m0, rows, final):
    bsz, seq, d = x.shape
    d_hy, d_ml = p["d_hy"], p["d_ml"]
    heads = ML_HEADS
    ctab, stab, sttab = tabs
    u_hy, u_ml, g = _inproj(x, mod, mod_row, p["norm_mix_w"], p["w_hy"], p["w_ml"], p["w_g"], p["b_g"])
    v, x0 = _hy_pre(u_hy, p["hy_conv_w"], p["hy_conv_b"])
    a, dd, ny = _filter_taps(seq, p["hy_filt"], d_hy)
    kre, kim = _filter_dft(ctab, stab, a, dd, ny)
    bt = math.gcd(bsz, 4)
    yre, yim = _dft_fwd(ctab, stab, v, kre, kim, bt)
    y_hy = _dft_inv(ctab, sttab, yre, yim, v, x0, p["hy_bias"], bt)
    qk = _ml_pre(u_ml, p["ml_conv_w"], p["ml_conv_b"], d_ml, d_ml // heads)
    m0b = jnp.broadcast_to(m0[..., None], m0.shape + (d_ml // heads,))
    h2dir, c_new, n_new, m_new = _mlstm(qk, u_ml, g, c0, n0, m0b, d_ml)
    x1, h2 = _mix_out(y_hy, h2dir, u_ml, x, mod, mod_row, p["ml_norm_w"], p["w_out"], p["norm_ffn_w"])
    x2 = _ffn(h2, x1, mod, mod_row, p["ffn_w_up"], p["ffn_conv_w"], p["ffn_conv_b"],
              p["ffn_w_down"], p["final_norm_w"], rows, final)
    return x2, c_new, n_new, m_new[..., 0]


def _layer_params(l, d_hy, d_ml, norm_mix_w, w_in, b_gate, hy_conv_w, hy_conv_b, hy_filt, hy_bias,
                  ml_conv_w, ml_conv_b, ml_norm_w, w_out, norm_ffn_w, ffn_w_up, ffn_conv_w,
                  ffn_conv_b, ffn_w_down, final_norm_w):
    w = w_in[l]
    n_gates = b_gate.shape[-1]
    d_ff = ffn_w_down.shape[1]
    wg = jnp.pad(w[:, 3 * d_hy + 4 * d_ml:], ((0, 0), (0, LANE - n_gates)))
    return {
        "d_hy": d_hy, "d_ml": d_ml,
        "norm_mix_w": norm_mix_w[l][None, :],
        "w_hy": w[:, :3 * d_hy].astype(BF16),
        "w_ml": w[:, 3 * d_hy:3 * d_hy + 4 * d_ml].astype(BF16),
        "w_g": jnp.concatenate(_split_hi_lo(wg), axis=1),
        "b_g": jnp.pad(b_gate[l], (0, LANE - n_gates))[None, :],
        "hy_conv_w": hy_conv_w[l], "hy_conv_b": hy_conv_b[l][None, :],
        "hy_filt": tuple(a[l] for a in hy_filt),
        "hy_bias": hy_bias[l][None, :],
        "ml_conv_w": ml_conv_w[l], "ml_conv_b": ml_conv_b[l][None, :],
        "ml_norm_w": ml_norm_w[l][None, :],
        "w_out": w_out[l].astype(BF16),
        "norm_ffn_w": norm_ffn_w[l][None, :],
        "ffn_w_up": ffn_w_up[l].astype(BF16),
        "ffn_conv_w": ffn_conv_w[l].reshape(9, 2 * d_ff),
        "ffn_conv_b": ffn_conv_b[l][None, :],
        "ffn_w_down": ffn_w_down[l].astype(BF16),
        "final_norm_w": final_norm_w[None, :],
    }


def kernel(x_prompt, x_sample, state_mlstm_C, state_mlstm_n, state_mlstm_m, c, c_ctx, ada_w, ada_b, norm_mix_w, w_in, b_gate, hy_conv_w, hy_conv_b, hy_filt_w1, hy_filt_b1, hy_filt_w2, hy_filt_b2, hy_filt_w3, hy_filt_b3, hy_filt_w4, hy_freq, hy_bias, ml_conv_w, ml_conv_b, ml_norm_w, w_out, norm_ffn_w, ffn_w_up, ffn_conv_w, ffn_conv_b, ffn_w_down, final_norm_w):
    depth, d, _ = ada_w.shape
    bp, seq_p, _ = x_prompt.shape
    bs, seq_s, _ = x_sample.shape
    heads = ML_HEADS
    d_hy = hy_bias.shape[-1]
    d_ml = ml_norm_w.shape[-1]
    dh = d_ml // heads
    assert bs < MOD_ROWS and seq_s % GRID_W == 0

    cvec = jnp.concatenate([c, c_ctx[None, :], jnp.zeros((MOD_ROWS - bs - 1, d), F32)], axis=0)
    mod_all = _ada(cvec, ada_w, ada_b).reshape(depth, MOD_ROWS, 6, d)
    row_ctx = lambda b: bs
    row_lat = lambda b: b

    tabs_p = _dft_tables(seq_p)
    tabs_s = _dft_tables(seq_s)
    zc = jnp.zeros((bp, 2, heads, dh, dh), F32)
    zn = jnp.zeros((bp, 2, heads, dh), F32)
    zm = jnp.zeros((bp, 2, heads), F32)
    hy_filt = (hy_filt_w1, hy_filt_b1, hy_filt_w2, hy_filt_b2, hy_filt_w3, hy_filt_b3, hy_filt_w4, hy_freq)

    xp, xs = x_prompt, x_sample
    new_c, new_n, new_m = [], [], []
    for l in range(depth):
        p = _layer_params(l, d_hy, d_ml, norm_mix_w, w_in, b_gate, hy_conv_w, hy_conv_b, hy_filt,
                          hy_bias, ml_conv_w, ml_conv_b, ml_norm_w, w_out, norm_ffn_w, ffn_w_up,
                          ffn_conv_w, ffn_conv_b, ffn_w_down, final_norm_w)
        final = l == depth - 1
        xp, cl, nl, ml = _layer(xp, mod_all[l], row_ctx, p, tabs_p, zc, zn, zm, 1, final)
        new_c.append(cl)
        new_n.append(nl)
        new_m.append(ml)
        xs, _, _, _ = _layer(xs, mod_all[l], row_lat, p, tabs_s, state_mlstm_C[:, l],
                             state_mlstm_n[:, l], state_mlstm_m[:, l], seq_s // GRID_W, final)
    return (xp, xs, jnp.stack(new_c, axis=1), jnp.stack(new_n, axis=1), jnp.stack(new_m, axis=1))
```

```python
import functools
import math

import numpy as np
import jax
import jax.numpy as jnp
from jax import lax
from jax.experimental import pallas as pl
from jax.experimental.pallas import tpu as pltpu

F32 = jnp.float32
BF16 = jnp.bfloat16
HIGHEST = lax.Precision.HIGHEST

GRID_W = 64
ML_HEADS = 4
CHUNK = 128
HY_BANDS = 16
HY_DECAY_TARGET = 1e-2
HY_FAST_DECAY_PCT = 0.3
HY_SLOW_DECAY_PCT = 1.5
HY_MIN_DECAY = math.log(HY_DECAY_TARGET) / HY_SLOW_DECAY_PCT
HY_MAX_DECAY = math.log(HY_DECAY_TARGET) / HY_FAST_DECAY_PCT
EPS = 1e-6

LANE = 128
HALO_ROWS = 16
VMEM_LIMIT = 56 * 1024 * 1024
MOD_ROWS = 8
DFT_ROW_TILE = 64
FFN_TILE = 256


def _params(*sem):
    return pltpu.CompilerParams(dimension_semantics=sem, vmem_limit_bytes=VMEM_LIMIT)


def _silu(x):
    return x * jax.nn.sigmoid(x)


def _rms(x, w):
    return x * lax.rsqrt(jnp.mean(x * x, axis=-1, keepdims=True) + EPS) * w


def _split_hi_lo(x):
    hi = x.astype(BF16)
    return hi, (x - hi.astype(F32)).astype(BF16)


def _ada_kernel(c_ref, w_ref, b_ref, o_ref):
    s = _silu(c_ref[...])
    o_ref[0] = jnp.dot(s, w_ref[0], precision=HIGHEST, preferred_element_type=F32) + b_ref[0]


def _ada(cvec, ada_w, ada_b):
    depth, d, nmod = ada_w.shape
    tn = 1024
    return pl.pallas_call(
        _ada_kernel,
        grid=(depth, nmod // tn),
        in_specs=[
            pl.BlockSpec((MOD_ROWS, d), lambda l, j: (0, 0)),
            pl.BlockSpec((1, d, tn), lambda l, j: (l, 0, j)),
            pl.BlockSpec((1, 1, tn), lambda l, j: (l, 0, j)),
        ],
        out_specs=pl.BlockSpec((1, MOD_ROWS, tn), lambda l, j: (l, 0, j)),
        out_shape=jax.ShapeDtypeStruct((depth, MOD_ROWS, nmod), F32),
        compiler_params=_params("parallel", "parallel"),
        name="ada_mod",
    )(cvec, ada_w, ada_b.reshape(depth, 1, nmod))


def _inproj_kernel(x_ref, mod_ref, nw_ref, why_ref, wml_ref, wg_ref, bg_ref,
                   uhy_ref, uml_ref, g_ref):
    m = mod_ref[0]
    h = _rms(x_ref[0], nw_ref[...]) * (1.0 + m[1:2]) + m[0:1]
    h_hi, h_lo = _split_hi_lo(h)
    uhy_ref[0] = jnp.dot(h_hi, why_ref[...], preferred_element_type=F32).astype(uhy_ref.dtype)
    uml_ref[0] = jnp.dot(h_hi, wml_ref[...], preferred_element_type=F32).astype(uml_ref.dtype)
    wg = wg_ref[...]
    ng = g_ref.shape[-1]
    a = jnp.dot(h_hi, wg, preferred_element_type=F32)
    b = jnp.dot(h_lo, wg[:, :ng], preferred_element_type=F32)
    g_ref[0] = a[:, :ng] + a[:, ng:] + b + bg_ref[...]


def _inproj(x, mod, mod_row, nw, w_hy, w_ml, w_g, b_g):
    bsz, seq, d = x.shape
    tb = min(seq, 512)
    n_hy, n_ml = w_hy.shape[1], w_ml.shape[1]
    tok = lambda n: pl.BlockSpec((1, tb, n), lambda b, i: (b, i, 0))
    full = lambda a: pl.BlockSpec(a.shape, lambda b, i: (0,) * a.ndim)
    return pl.pallas_call(
        _inproj_kernel,
        grid=(bsz, seq // tb),
        in_specs=[tok(d), pl.BlockSpec((1, 6, d), lambda b, i: (mod_row(b), 0, 0)),
                  full(nw), full(w_hy), full(w_ml), full(w_g), full(b_g)],
        out_specs=[tok(n_hy), tok(n_ml), tok(LANE)],
        out_shape=[jax.ShapeDtypeStruct((bsz, seq, n_hy), BF16),
                   jax.ShapeDtypeStruct((bsz, seq, n_ml), BF16),
                   jax.ShapeDtypeStruct((bsz, seq, LANE), F32)],
        compiler_params=_params("parallel", "parallel"),
        name="inproj",
    )(x, mod, nw, w_hy, w_ml, w_g, b_g)


def _dwconv3(u, before, after, w, b):
    n = u.shape[0]
    row = lax.broadcasted_iota(jnp.int32, u.shape, 0)
    prev = jnp.where(row == 0, before, pltpu.roll(u, 1, 0))
    nxt = jnp.where(row == n - 1, after, pltpu.roll(u, n - 1, 0))
    return prev * w[0:1] + u * w[1:2] + nxt * w[2:3] + b


def _tile_neighbours(prev_ref, next_ref):
    i = pl.program_id(1)
    before = jnp.where(i == 0, 0.0, prev_ref[0].astype(F32)[HALO_ROWS - 1:HALO_ROWS])
    after = jnp.where(i == pl.num_programs(1) - 1, 0.0, next_ref[0].astype(F32)[0:1])
    return before, after


def _row_tile_specs(seq, tb, c, cblk=0):
    per = tb // HALO_ROWS
    last = seq // HALO_ROWS - 1
    main = pl.BlockSpec((1, tb, c), lambda b, i: (b, i, cblk))
    prev = pl.BlockSpec((1, HALO_ROWS, c), lambda b, i: (b, jnp.maximum(i * per - 1, 0), cblk))
    nxt = pl.BlockSpec((1, HALO_ROWS, c), lambda b, i: (b, jnp.minimum((i + 1) * per, last), cblk))
    return main, prev, nxt


def _hy_pre_kernel(u_ref, up_ref, un_ref, w_ref, b_ref, v_ref, x0_ref):
    before, after = _tile_neighbours(up_ref, un_ref)
    y = _dwconv3(u_ref[0].astype(F32), before, after, w_ref[...], b_ref[...])
    c = v_ref.shape[-1]
    x0_ref[0] = y[:, :c].astype(x0_ref.dtype)
    v_ref[0] = (y[:, 2 * c:] * y[:, c:2 * c]).astype(v_ref.dtype)


def _hy_pre(u_hy, conv_w, conv_b):
    bsz, seq, c3 = u_hy.shape
    tb = min(seq, 512)
    main, prev, nxt = _row_tile_specs(seq, tb, c3)
    full = lambda a: pl.BlockSpec(a.shape, lambda b, i: (0,) * a.ndim)
    out = pl.BlockSpec((1, tb, c3 // 3), lambda b, i: (b, i, 0))
    shp = jax.ShapeDtypeStruct((bsz, seq, c3 // 3), BF16)
    return pl.pallas_call(
        _hy_pre_kernel,
        grid=(bsz, seq // tb),
        in_specs=[main, prev, nxt, full(conv_w), full(conv_b)],
        out_specs=[out, out],
        out_shape=[shp, shp],
        compiler_params=_params("parallel", "parallel"),
        name="hyena_pre",
    )(u_hy, u_hy, u_hy, conv_w, conv_b)


def _ml_pre_kernel(u_ref, up_ref, un_ref, w_ref, b_ref, o_ref, *, d_ml, kscale):
    before, after = _tile_neighbours(up_ref, un_ref)
    y = _silu(_dwconv3(u_ref[0].astype(F32), before, after, w_ref[...], b_ref[...]))
    lane = lax.broadcasted_iota(jnp.int32, (1, y.shape[1]), 1)
    o_ref[0] = (y * jnp.where(lane >= d_ml, kscale, 1.0)).astype(o_ref.dtype)


def _ml_pre(u_ml, conv_w, conv_b, d_ml, dh):
    bsz, seq, _ = u_ml.shape
    tb = min(seq, 512)
    main, prev, nxt = _row_tile_specs(seq, tb, 2 * d_ml)
    full = lambda a: pl.BlockSpec(a.shape, lambda b, i: (0,) * a.ndim)
    return pl.pallas_call(
        functools.partial(_ml_pre_kernel, d_ml=d_ml, kscale=dh ** -0.5),
        grid=(bsz, seq // tb),
        in_specs=[main, prev, nxt, full(conv_w), full(conv_b)],
        out_specs=pl.BlockSpec((1, tb, 2 * d_ml), lambda b, i: (b, i, 0)),
        out_shape=jax.ShapeDtypeStruct((bsz, seq, 2 * d_ml), BF16),
        compiler_params=_params("parallel", "parallel"),
        name="mlstm_pre",
    )(u_ml, u_ml, u_ml, conv_w, conv_b)


def _dft_table_kernel(pre_ref, pim_ref, qre_ref, qim_ref, c_ref, s_ref, st_ref):
    pre, pim = pre_ref[0], pim_ref[0]
    qre, qim = qre_ref[...], qim_ref[...]
    c = pre * qre - pim * qim
    s = pre * qim + pim * qre
    grow = lax.broadcasted_iota(jnp.int32, c.shape, 0) + pl.program_id(0) * c.shape[0]
    lane = lax.broadcasted_iota(jnp.int32, c.shape, 1)
    alt_lane = (1 - 2 * (lane & 1)).astype(F32)
    alt_row = (1 - 2 * (grow & 1)).astype(F32)
    c_ref[...] = c.astype(BF16)
    s_ref[...] = jnp.where(grow == 0, alt_lane, s).astype(BF16)
    st_ref[...] = jnp.where(lane == 0, alt_row, s).astype(BF16)


def _dft_tables(seq):
    n = 2 * seq
    tm = DFT_ROW_TILE
    s = jnp.arange(seq, dtype=jnp.int32)[None, :]
    ang = lambda f: (-2.0 * math.pi / n) * ((f * s) % n).astype(F32)
    fp = (jnp.arange(seq // tm, dtype=jnp.int32) * tm)[:, None]
    fq = jnp.arange(tm, dtype=jnp.int32)[:, None]
    ap, aq = ang(fp), ang(fq)
    p3 = lambda a: a.reshape(seq // tm, 1, seq)
    pspec = pl.BlockSpec((1, 1, seq), lambda r: (r, 0, 0))
    qspec = pl.BlockSpec((tm, seq), lambda r: (0, 0))
    out = pl.BlockSpec((tm, seq), lambda r: (r, 0))
    shp = jax.ShapeDtypeStruct((seq, seq), BF16)
    return pl.pallas_call(
        _dft_table_kernel,
        grid=(seq // tm,),
        in_specs=[pspec, pspec, qspec, qspec],
        out_specs=[out, out, out],
        out_shape=[shp, shp, shp],
        compiler_params=_params("parallel"),
        name="dft_tables",
    )(p3(jnp.cos(ap)), p3(jnp.sin(ap)), jnp.cos(aq), jnp.sin(aq))


def _filter_kernel(z_ref, w1_ref, b1_ref, w2_ref, b2_ref, w3_ref, b3_ref, fr_ref,
                   w4f_ref, w4b_ref, dl_ref, a_ref, d_ref, ny_ref, h_ref):
    dot = functools.partial(jnp.dot, precision=HIGHEST, preferred_element_type=F32)

    @pl.when(pl.program_id(0) == 0)
    def _():
        fr = fr_ref[...]
        h = jnp.sin(fr * (dot(z_ref[...], w1_ref[...]) + b1_ref[...]))
        h = jnp.sin(fr * (dot(h, w2_ref[...]) + b2_ref[...]))
        h_ref[...] = jnp.sin(fr * (dot(h, w3_ref[...]) + b3_ref[...]))

    h_hi, h_lo = _split_hi_lo(h_ref[...])

    def dot3(w):
        w_hi, w_lo = _split_hi_lo(w)
        mm = functools.partial(jnp.dot, preferred_element_type=F32)
        return mm(h_hi, w_hi) + mm(h_hi, w_lo) + mm(h_lo, w_hi)

    decay = jnp.exp(-z_ref[:, 0:1] * dl_ref[...])
    hf = dot3(w4f_ref[...]) * decay
    hb = dot3(w4b_ref[...]) * decay
    row = lax.broadcasted_iota(jnp.int32, hf.shape, 0)
    hb = jnp.where(row == 0, 0.0, hb)
    inv = 1.0 / jnp.sum(jnp.abs(hf) + jnp.abs(hb), axis=0, keepdims=True)
    a = (hf + hb) * inv
    a_ref[...] = a.astype(BF16)
    d_ref[...] = ((hf - hb) * inv).astype(BF16)
    ny_ref[...] = jnp.sum(a * (1 - 2 * (row & 1)).astype(F32), axis=0, keepdims=True)


def _filter_taps(seq, filt, d_hy):
    w1, b1, w2, b2, w3, b3, w4, freq = filt
    emb, ffn = w1.shape
    t = jnp.linspace(0.0, 1.0, seq, dtype=F32)[:, None]
    wpos = (2.0 * math.pi / seq) * jnp.arange(seq, dtype=F32)[:, None]
    bands = jnp.linspace(1e-4, HY_BANDS - 1, HY_BANDS, dtype=F32)[None, :]
    z = jnp.concatenate([t, jnp.cos(bands * wpos), -jnp.sin(bands * wpos),
                         jnp.zeros((seq, LANE - emb), F32)], axis=-1)
    w1p = jnp.concatenate([w1, jnp.zeros((LANE - emb, ffn), F32)], axis=0)
    deltas = jnp.abs(jnp.linspace(HY_MIN_DECAY, HY_MAX_DECAY, d_hy, dtype=F32))[None, :]
    row = lambda a: a.reshape(1, -1)
    nblk = d_hy // LANE
    full = lambda a: pl.BlockSpec(a.shape, lambda j: (0,) * a.ndim)
    args = [z, w1p, row(b1), w2, row(b2), w3, row(b3), row(freq)]
    col = pl.BlockSpec((seq, LANE), lambda j: (0, j))
    vec = pl.BlockSpec((1, LANE), lambda j: (0, j))
    return pl.pallas_call(
        _filter_kernel,
        grid=(nblk,),
        in_specs=[full(a) for a in args] + [
            pl.BlockSpec((ffn, LANE), lambda j: (0, j)),
            pl.BlockSpec((ffn, LANE), lambda j: (0, nblk + j)), vec],
        out_specs=[col, col, vec],
        out_shape=[jax.ShapeDtypeStruct((seq, d_hy), BF16),
                   jax.ShapeDtypeStruct((seq, d_hy), BF16),
                   jax.ShapeDtypeStruct((1, d_hy), F32)],
        scratch_shapes=[pltpu.VMEM((seq, ffn), F32)],
        compiler_params=_params("arbitrary"),
        name="hyena_filter",
    )(*args, w4, w4, deltas)


def _filter_dft_kernel(c_ref, s_ref, a_ref, d_ref, ny_ref, kre_ref, kim_ref, are_ref, aim_ref):
    k = pl.program_id(1)

    @pl.when(k == 0)
    def _():
        are_ref[...] = jnp.zeros_like(are_ref)
        aim_ref[...] = jnp.zeros_like(aim_ref)

    are_ref[...] += jnp.dot(c_ref[...], a_ref[...], preferred_element_type=F32)
    aim_ref[...] += jnp.dot(s_ref[...], d_ref[...], preferred_element_type=F32)

    @pl.when(k == pl.num_programs(1) - 1)
    def _():
        grow = lax.broadcasted_iota(jnp.int32, are_ref.shape, 0) + pl.program_id(0) * are_ref.shape[0]
        kre_ref[...] = are_ref[...]
        kim_ref[...] = jnp.where(grow == 0, ny_ref[...], aim_ref[...])


def _filter_dft(ctab, stab, a, d, ny):
    seq, c = a.shape
    t = min(seq, 1024)
    return pl.pallas_call(
        _filter_dft_kernel,
        grid=(seq // t, seq // t),
        in_specs=[pl.BlockSpec((t, t), lambda m, k: (m, k)),
                  pl.BlockSpec((t, t), lambda m, k: (m, k)),
                  pl.BlockSpec((t, c), lambda m, k: (k, 0)),
                  pl.BlockSpec((t, c), lambda m, k: (k, 0)),
                  pl.BlockSpec((1, c), lambda m, k: (0, 0))],
        out_specs=[pl.BlockSpec((t, c), lambda m, k: (m, 0))] * 2,
        out_shape=[jax.ShapeDtypeStruct((seq, c), F32)] * 2,
        scratch_shapes=[pltpu.VMEM((t, c), F32), pltpu.VMEM((t, c), F32)],
        compiler_params=_params("parallel", "arbitrary"),
        name="hyena_filter_dft",
    )(ctab, stab, a, d, ny)


def _dft_fwd_kernel(c_ref, s_ref, v_ref, kre_ref, kim_ref, yre_ref, yim_ref, are_ref, aim_ref):
    k = pl.program_id(2)
    bt = v_ref.shape[0]

    @pl.when(k == 0)
    def _():
        are_ref[...] = jnp.zeros_like(are_ref)
        aim_ref[...] = jnp.zeros_like(aim_ref)

    for b in range(bt):
        vb = v_ref[b].astype(BF16)
        are_ref[b] += jnp.dot(c_ref[...], vb, preferred_element_type=F32)
        aim_ref[b] += jnp.dot(s_ref[...], vb, preferred_element_type=F32)

    @pl.when(k == pl.num_programs(2) - 1)
    def _():
        kre, kim = kre_ref[...], kim_ref[...]
        grow = lax.broadcasted_iota(jnp.int32, kre.shape, 0) + pl.program_id(1) * kre.shape[0]
        packed = grow == 0
        for b in range(bt):
            xr, xi = are_ref[b], aim_ref[b]
            yre = jnp.where(packed, 0.5 * xr * kre, xr * kre - xi * kim)
            yim = jnp.where(packed, 0.5 * xi * kim, xr * kim + xi * kre)
            yre_ref[b] = yre.astype(BF16)
            yim_ref[b] = yim.astype(BF16)


def _dft_fwd(ctab, stab, v, kre, kim, bt):
    bsz, seq, c = v.shape
    t = min(seq, 512)
    tab = pl.BlockSpec((t, t), lambda b, m, k: (m, k))
    kspec = pl.BlockSpec((t, c), lambda b, m, k: (m, 0))
    out = pl.BlockSpec((bt, t, c), lambda b, m, k: (b, m, 0))
    shp = jax.ShapeDtypeStruct((bsz, seq, c), BF16)
    return pl.pallas_call(
        _dft_fwd_kernel,
        grid=(bsz // bt, seq // t, seq // t),
        in_specs=[tab, tab, pl.BlockSpec((bt, t, c), lambda b, m, k: (b, k, 0)), kspec, kspec],
        out_specs=[out, out],
        out_shape=[shp, shp],
        scratch_shapes=[pltpu.VMEM((bt, t, c), F32), pltpu.VMEM((bt, t, c), F32)],
        compiler_params=_params("parallel", "parallel", "arbitrary"),
        name="hyena_dft_fwd",
    )(ctab, stab, v, kre, kim)


def _dft_inv_kernel(c_ref, st_ref, yre_ref, yim_ref, v_ref, x0_ref, hb_ref, o_ref, acc_ref, *, scale):
    k = pl.program_id(2)
    bt = v_ref.shape[0]

    @pl.when(k == 0)
    def _():
        acc_ref[...] = jnp.zeros_like(acc_ref)

    for b in range(bt):
        acc_ref[b] += (jnp.dot(c_ref[...], yre_ref[b], preferred_element_type=F32)
                       + jnp.dot(st_ref[...], yim_ref[b], preferred_element_type=F32))

    @pl.when(k == pl.num_programs(2) - 1)
    def _():
        for b in range(bt):
            v = v_ref[b]
            y = (acc_ref[b] * scale + hb_ref[...] * v.astype(F32)) * x0_ref[b].astype(F32)
            o_ref[b] = y.astype(o_ref.dtype)


def _dft_inv(ctab, sttab, yre, yim, v, x0, hy_bias, bt):
    bsz, seq, c = v.shape
    t = min(seq, 512)
    tab = pl.BlockSpec((t, t), lambda b, m, k: (m, k))
    yspec = pl.BlockSpec((bt, t, c), lambda b, m, k: (b, k, 0))
    tok = pl.BlockSpec((bt, t, c), lambda b, m, k: (b, m, 0))
    return pl.pallas_call(
        functools.partial(_dft_inv_kernel, scale=1.0 / seq),
        grid=(bsz // bt, seq // t, seq // t),
        in_specs=[tab, tab, yspec, yspec, tok, tok, pl.BlockSpec((1, c), lambda b, m, k: (0, 0))],
        out_specs=tok,
        out_shape=jax.ShapeDtypeStruct((bsz, seq, c), BF16),
        scratch_shapes=[pltpu.VMEM((bt, t, c), F32)],
        compiler_params=_params("parallel", "parallel", "arbitrary"),
        name="hyena_dft_inv",
    )(ctab, sttab, yre, yim, v, x0, hy_bias)


def _log_sigmoid(x):
    return jnp.minimum(x, 0.0) - jnp.log1p(jnp.exp(-jnp.abs(x)))


def _mlstm_kernel(q_ref, k_ref, v_ref, g_ref, gt_ref, c0_ref, n0_ref, m0_ref,
                  h_ref, c_out_ref, n_out_ref, m_out_ref, c_s, n_s, m_s, *, heads, dh):
    step = pl.program_id(2)
    fwd = pl.program_id(0) == 0
    bt, t = q_ref.shape[0], q_ref.shape[1]

    @pl.when(step == 0)
    def _():
        c_s[...] = c0_ref[:, 0]
        n_s[...] = n0_ref[:, 0]
        m_s[...] = m0_ref[:, 0]

    row = lax.broadcasted_iota(jnp.int32, (t, t), 0)
    col = lax.broadcasted_iota(jnp.int32, (t, t), 1)
    sgn = jnp.where(fwd, 1, -1)
    seen = sgn * (row - col) >= 0
    seen_b = seen.astype(BF16)
    seen_tb = (sgn * (col - row) >= 0).astype(BF16)
    nt = (((1,), (1,)), ((), ()))
    tn = (((0,), (0,)), ((), ()))

    def split3(x, axis):
        hi = x.astype(BF16).astype(F32)
        r = x - hi
        mid = r.astype(BF16).astype(F32)
        return jnp.concatenate([hi, mid, r - mid], axis=axis).astype(BF16)

    g = g_ref[0, 0]
    gt = gt_ref[0, 0]
    lf = _log_sigmoid(g)
    lf_tot = jnp.sum(lf, axis=0, keepdims=True)
    cs = jnp.dot(seen_b, split3(lf, 1), preferred_element_type=F32)
    nl = lf.shape[1]
    bcol_all = cs[:, :nl] + cs[:, nl:2 * nl] + cs[:, 2 * nl:]
    rs = jnp.dot(split3(_log_sigmoid(gt), 0), seen_tb, preferred_element_type=F32)
    nr = gt.shape[0]
    brow_all = rs[:nr] + rs[nr:2 * nr] + rs[2 * nr:]
    chains = [(b, hd) for b in range(bt) for hd in range(heads)]
    sl = lambda hd: slice(hd * dh, (hd + 1) * dh)
    qs = [q_ref[b, :, sl(hd)] for b, hd in chains]
    ks = [k_ref[b, :, sl(hd)] for b, hd in chains]
    vs = [v_ref[b, :, sl(hd)].astype(BF16) for b, hd in chains]
    qk = [lax.dot_general(q, k, nt, preferred_element_type=F32) for q, k in zip(qs, ks)]
    each = lambda f, *ls: [f(*a) for a in zip(*ls)]
    fcols = [b * 2 * heads + heads + hd for b, hd in chains]
    bc = [bcol_all[:, c:c + 1] for c in fcols]
    rowt = [gt[c - heads:c - heads + 1, :] - brow_all[c:c + 1, :] for c in fcols]
    ic = [g[:, c - heads:c - heads + 1] for c in fcols]
    b_end = [lf_tot[:, c:c + 1] for c in fcols]
    m_prev = [m_s[b, hd:hd + 1, 0:1] for b, hd in chains]
    ct_prev = [c_s[b, hd] for b, hd in chains]
    n_prev = [n_s[b, hd:hd + 1, :] for b, hd in chains]
    dmat = each(lambda x, y: jnp.where(seen, x + y, -jnp.inf), bc, rowt)
    inter = each(lambda x, m: x + m, bc, m_prev)
    dmax = each(lambda d: jnp.max(d, axis=1, keepdims=True), dmat)
    m_t = each(jnp.maximum, inter, dmax)
    s = each(lambda r, d, m: r * jnp.exp(d - m), qk, dmat, m_t)
    w_inter = each(lambda i, m: jnp.exp(i - m), inter, m_t)
    qf = each(lambda q: q.astype(F32), qs)
    lhs = each(lambda x, q, w: jnp.concatenate([x.astype(BF16), (q * w).astype(BF16)], axis=1),
               s, qf, w_inter)
    rhs = each(lambda v, c: jnp.concatenate([v, c.astype(BF16)], axis=0), vs, ct_prev)
    nums = each(lambda l, r: jnp.dot(l, r, preferred_element_type=F32), lhs, rhs)
    qn = each(lambda q, n: jnp.sum(q * n, axis=1, keepdims=True), qf, n_prev)
    ssum = each(lambda x: jnp.sum(x, axis=1, keepdims=True), s)
    den = each(lambda a, w, x: a + w * x, ssum, w_inter, qn)
    w_log = each(lambda e, x, i: e - x + i, b_end, bc, ic)
    wmax = each(lambda w: jnp.max(w, axis=0, keepdims=True), w_log)
    m_new = each(lambda e, m, w: jnp.maximum(e + m, w), b_end, m_prev, wmax)
    w_s = each(lambda w, m: jnp.exp(w - m), w_log, m_new)
    w_c = each(lambda e, m, mn: jnp.exp(e + m - mn), b_end, m_prev, m_new)
    kw = each(lambda k, w: k.astype(F32) * w, ks, w_s)
    upds = each(lambda x, v: lax.dot_general(x.astype(BF16), v, tn, preferred_element_type=F32), kw, vs)
    ksum = each(lambda x: jnp.sum(x, axis=0, keepdims=True), kw)
    for i, (b, hd) in enumerate(chains):
        h_ref[0, b, :, sl(hd)] = nums[i] / jnp.maximum(jnp.abs(den[i]), jnp.exp(-m_t[i]))
        c_s[b, hd] = w_c[i] * ct_prev[i] + upds[i]
        n_s[b, hd:hd + 1, :] = w_c[i] * n_prev[i] + ksum[i]
        m_s[b, hd:hd + 1, :] = jnp.broadcast_to(m_new[i], (1, dh))

    @pl.when(step == pl.num_programs(2) - 1)
    def _():
        c_out_ref[:, 0] = c_s[...]
        n_out_ref[:, 0] = n_s[...]
        m_out_ref[:, 0] = m_s[...]


def _mlstm(qk, u_ml, g, c0, n0, m0, d_ml):
    bsz, seq, _ = qk.shape
    heads = ML_HEADS
    dh = d_ml // heads
    nc = seq // CHUNK
    bt = math.gcd(bsz, 4)
    nbb = bsz // bt
    gi = g[..., :4 * heads].reshape(nbb, bt, seq, 2, 2 * heads)
    g2 = jnp.transpose(gi, (3, 0, 2, 1, 4)).reshape(2, nbb, seq, bt * 2 * heads)
    gt2 = jnp.swapaxes(g2, 2, 3)
    g2 = jnp.pad(g2, ((0, 0), (0, 0), (0, 0), (0, LANE - bt * 2 * heads)))
    chunk = lambda d, c: c + d * (nc - 1 - 2 * c)
    tok = lambda j: pl.BlockSpec((bt, CHUNK, d_ml), lambda d, b, c: (b, chunk(d, c), j))
    st = lambda *tail: pl.BlockSpec((bt, 1, heads) + tail, lambda d, b, c: (b, d, 0) + (0,) * len(tail))
    h, ct, n, m = pl.pallas_call(
        functools.partial(_mlstm_kernel, heads=heads, dh=dh),
        grid=(2, nbb, nc),
        in_specs=[tok(0), tok(1), tok(2),
                  pl.BlockSpec((1, 1, CHUNK, LANE), lambda d, b, c: (d, b, chunk(d, c), 0)),
                  pl.BlockSpec((1, 1, bt * 2 * heads, CHUNK), lambda d, b, c: (d, b, 0, chunk(d, c))),
                  st(dh, dh), st(dh), st(dh)],
        out_specs=[pl.BlockSpec((1, bt, CHUNK, d_ml), lambda d, b, c: (d, b, chunk(d, c), 0)),
                   st(dh, dh), st(dh), st(dh)],
        out_shape=[jax.ShapeDtypeStruct((2, bsz, seq, d_ml), F32),
                   jax.ShapeDtypeStruct((bsz, 2, heads, dh, dh), F32),
                   jax.ShapeDtypeStruct((bsz, 2, heads, dh), F32),
                   jax.ShapeDtypeStruct((bsz, 2, heads, dh), F32)],
        scratch_shapes=[pltpu.VMEM((bt, heads, dh, dh), F32), pltpu.VMEM((bt, heads, dh), F32),
                        pltpu.VMEM((bt, heads, dh), F32)],
        compiler_params=_params("parallel", "parallel", "arbitrary"),
        name="mlstm_scan",
    )(qk, qk, u_ml, g2, gt2, jnp.swapaxes(c0, -1, -2), n0, m0)
    return h, jnp.swapaxes(ct, -1, -2), n, m


def _mix_out_kernel(yhy_ref, hf_ref, hb_ref, o_ref, x_ref, mod_ref, mlw_ref, wout_ref, fnw_ref,
                    x1_ref, h2_ref, *, heads, dh):
    m = mod_ref[0]
    h = hf_ref[0, 0] + hb_ref[0, 0]
    o = o_ref[0].astype(F32)
    mlw = mlw_ref[...]
    parts = [yhy_ref[0].astype(BF16)]
    for hd in range(heads):
        sl = slice(hd * dh, (hd + 1) * dh)
        parts.append((_rms(h[:, sl], mlw[:, sl]) * jax.nn.sigmoid(o[:, sl])).astype(BF16))
    y = jnp.concatenate(parts, axis=-1)
    x1 = x_ref[0] + m[2:3] * jnp.dot(y, wout_ref[...], preferred_element_type=F32)
    x1_ref[0] = x1
    h2_ref[0] = (_rms(x1, fnw_ref[...]) * (1.0 + m[4:5]) + m[3:4]).astype(BF16)


def _mix_out(y_hy, h2dir, u_ml, x, mod, mod_row, ml_norm_w, w_out, norm_ffn_w):
    bsz, seq, d = x.shape
    d_ml = h2dir.shape[-1]
    tb = min(seq, 512)
    tok = lambda n, j=0: pl.BlockSpec((1, tb, n), lambda b, i: (b, i, j))
    hdir = lambda dr: pl.BlockSpec((1, 1, tb, d_ml), lambda b, i: (dr, b, i, 0))
    full = lambda a: pl.BlockSpec(a.shape, lambda b, i: (0,) * a.ndim)
    return pl.pallas_call(
        functools.partial(_mix_out_kernel, heads=ML_HEADS, dh=d_ml // ML_HEADS),
        grid=(bsz, seq // tb),
        in_specs=[tok(y_hy.shape[-1]), hdir(0), hdir(1), tok(d_ml, 3), tok(d),
                  pl.BlockSpec((1, 6, d), lambda b, i: (mod_row(b), 0, 0)),
                  full(ml_norm_w), full(w_out), full(norm_ffn_w)],
        out_specs=[tok(d), tok(d)],
        out_shape=[jax.ShapeDtypeStruct((bsz, seq, d), F32),
                   jax.ShapeDtypeStruct((bsz, seq, d), BF16)],
        compiler_params=_params("parallel", "parallel"),
        name="mixer_out",
    )(y_hy, h2dir, h2dir, u_ml, x, mod, ml_norm_w, w_out, norm_ffn_w)


def _ffn_kernel(*refs, width, halo, final, nj):
    if halo:
        (top_ref, main_ref, bot_ref, x1_ref, mod_ref, wup_ref, cw_ref, cb_ref, wd_ref, fw_ref,
         o_ref, acc_ref) = refs
    else:
        (main_ref, x1_ref, mod_ref, wup_ref, cw_ref, cb_ref, wd_ref, fw_ref, o_ref, acc_ref) = refs
    d = main_ref.shape[-1]
    hm = main_ref[...].reshape(-1, d)
    m_rows = hm.shape[0]
    acc_ref[...] = jnp.zeros_like(acc_ref)

    col = lax.broadcasted_iota(jnp.int32, (m_rows, 1), 0) % width
    not_first = (col != 0).astype(F32)
    not_last = (col != width - 1).astype(F32)
    if halo:
        r = pl.program_id(1)
        top_ok = jnp.where(r > 0, 1.0, 0.0)
        bot_ok = jnp.where(r < pl.num_programs(1) - 1, 1.0, 0.0)
        ht, hb = top_ref[0], bot_ref[0]

    def conv(w, cw, cb):
        um = jnp.dot(hm, w, preferred_element_type=F32)
        slabs = [(1, um)]
        if halo:
            ut = jnp.dot(ht, w, preferred_element_type=F32) * top_ok
            ub = jnp.dot(hb, w, preferred_element_type=F32) * bot_ok
            full = jnp.concatenate([ut, um, ub], axis=0)
            slabs = [(kr, full[kr * width:kr * width + m_rows]) for kr in range(3)]
        left = sum(s * cw[3 * kr:3 * kr + 1] for kr, s in slabs)
        mid = sum(s * cw[3 * kr + 1:3 * kr + 2] for kr, s in slabs)
        right = sum(s * cw[3 * kr + 2:3 * kr + 3] for kr, s in slabs)
        return (mid + not_first * pltpu.roll(left, 1, 0)
                + not_last * pltpu.roll(right, m_rows - 1, 0) + cb)

    def tile(j, carry):
        a = conv(wup_ref[j], cw_ref[j], cb_ref[j])
        val = conv(wup_ref[nj + j], cw_ref[nj + j], cb_ref[nj + j])
        act = (_silu(a) * val).astype(BF16)
        acc_ref[...] += jnp.dot(act, wd_ref[j], preferred_element_type=F32)
        return carry

    lax.fori_loop(0, nj, tile, 0)
    m = mod_ref[0]
    x2 = x1_ref[...].reshape(-1, d) + m[5:6] * acc_ref[...]
    if final:
        x2 = _rms(x2, fw_ref[...])
    o_ref[...] = x2.reshape(o_ref.shape)


def _ffn(h2, x1, mod, mod_row, w_up, conv_w, conv_b, w_down, final_w, rows, final):
    bsz, seq, d = x1.shape
    nj = w_down.shape[0]
    width = seq // rows
    halo = rows > 1
    if halo:
        rb = 16
        nb, m_tok, nr = 1, rb * width, rows // rb
    else:
        nb, m_tok, nr = math.gcd(bsz, 4), seq, 1
    tok = pl.BlockSpec((nb, m_tok, d), lambda b, r: (b, r, 0))
    whole = lambda a: pl.BlockSpec(a.shape, lambda b, r: (0,) * a.ndim, pipeline_mode=pl.Buffered(1))
    in_specs, args = [], []
    if halo:
        in_specs += [pl.BlockSpec((1, width, d), lambda b, r: (b, jnp.maximum(r * rb - 1, 0), 0)),
                     tok,
                     pl.BlockSpec((1, width, d), lambda b, r: (b, jnp.minimum((r + 1) * rb, rows - 1), 0))]
        args += [h2, h2, h2]
    else:
        in_specs += [tok]
        args += [h2]
    in_specs += [tok, pl.BlockSpec((1, 6, d), lambda b, r: (mod_row(b), 0, 0)),
                 whole(w_up), whole(conv_w), whole(conv_b), whole(w_down), whole(final_w)]
    args += [x1, mod, w_up, conv_w, conv_b, w_down, final_w]
    return pl.pallas_call(
        functools.partial(_ffn_kernel, width=width, halo=halo, final=final, nj=nj),
        grid=(bsz // nb, nr),
        in_specs=in_specs,
        out_specs=tok,
        out_shape=jax.ShapeDtypeStruct((bsz, seq, d), F32),
        scratch_shapes=[pltpu.VMEM((nb * m_tok, d), F32)],
        compiler_params=_params("parallel", "parallel"),
        name="conv_ffn",
    )(*args)


def _layer(x, mod, mod_row, p, tabs, c0, n0, m0, rows, final):
    bsz, seq, d = x.shape
    d_hy, d_ml = p["d_hy"], p["d_ml"]
    heads = ML_HEADS
    ctab, stab, sttab = tabs
    u_hy, u_ml, g = _inproj(x, mod, mod_row, p["norm_mix_w"], p["w_hy"], p["w_ml"], p["w_g"], p["b_g"])
    v, x0 = _hy_pre(u_hy, p["hy_conv_w"], p["hy_conv_b"])
    a, dd, ny = _filter_taps(seq, p["hy_filt"], d_hy)
    kre, kim = _filter_dft(ctab, stab, a, dd, ny)
    bt = math.gcd(bsz, 4)
    yre, yim = _dft_fwd(ctab, stab, v, kre, kim, bt)
    y_hy = _dft_inv(ctab, sttab, yre, yim, v, x0, p["hy_bias"], bt)
    qk = _ml_pre(u_ml, p["ml_conv_w"], p["ml_conv_b"], d_ml, d_ml // heads)
    m0b = jnp.broadcast_to(m0[..., None], m0.shape + (d_ml // heads,))
    h2dir, c_new, n_new, m_new = _mlstm(qk, u_ml, g, c0, n0, m0b, d_ml)
    x1, h2 = _mix_out(y_hy, h2dir, u_ml, x, mod, mod_row, p["ml_norm_w"], p["w_out"], p["norm_ffn_w"])
    x2 = _ffn(h2, x1, mod, mod_row, p["ffn_w_up"], p["ffn_conv_w"], p["ffn_conv_b"],
              p["ffn_w_down"], p["final_norm_w"], rows, final)
    return x2, c_new, n_new, m_new[..., 0]


def _layer_params(l, d_hy, d_ml, norm_mix_w, w_in, b_gate, hy_conv_w, hy_conv_b, hy_filt, hy_bias,
                  ml_conv_w, ml_conv_b, ml_norm_w, w_out, norm_ffn_w, ffn_w_up, ffn_conv_w,
                  ffn_conv_b, ffn_w_down, final_norm_w):
    w = w_in[l]
    n_gates = b_gate.shape[-1]
    d_ff = ffn_w_down.shape[1]
    wg = jnp.pad(w[:, 3 * d_hy + 4 * d_ml:], ((0, 0), (0, LANE - n_gates)))
    nj = d_ff // FFN_TILE
    tiles = lambda a: jnp.transpose(a.reshape(a.shape[0], 2 * nj, FFN_TILE), (1, 0, 2))
    return {
        "d_hy": d_hy, "d_ml": d_ml,
        "norm_mix_w": norm_mix_w[l][None, :],
        "w_hy": w[:, :3 * d_hy].astype(BF16),
        "w_ml": w[:, 3 * d_hy:3 * d_hy + 4 * d_ml].astype(BF16),
        "w_g": jnp.concatenate(_split_hi_lo(wg), axis=1),
        "b_g": jnp.pad(b_gate[l], (0, LANE - n_gates))[None, :],
        "hy_conv_w": hy_conv_w[l], "hy_conv_b": hy_conv_b[l][None, :],
        "hy_filt": tuple(a[l] for a in hy_filt),
        "hy_bias": hy_bias[l][None, :],
        "ml_conv_w": ml_conv_w[l], "ml_conv_b": ml_conv_b[l][None, :],
        "ml_norm_w": ml_norm_w[l][None, :],
        "w_out": w_out[l].astype(BF16),
        "norm_ffn_w": norm_ffn_w[l][None, :],
        "ffn_w_up": tiles(ffn_w_up[l]).astype(BF16),
        "ffn_conv_w": tiles(ffn_conv_w[l].reshape(9, 2 * d_ff)),
        "ffn_conv_b": ffn_conv_b[l].reshape(2 * nj, 1, FFN_TILE),
        "ffn_w_down": ffn_w_down[l].astype(BF16).reshape(nj, FFN_TILE, -1),
        "final_norm_w": final_norm_w[None, :],
    }


def kernel(x_prompt, x_sample, state_mlstm_C, state_mlstm_n, state_mlstm_m, c, c_ctx, ada_w, ada_b, norm_mix_w, w_in, b_gate, hy_conv_w, hy_conv_b, hy_filt_w1, hy_filt_b1, hy_filt_w2, hy_filt_b2, hy_filt_w3, hy_filt_b3, hy_filt_w4, hy_freq, hy_bias, ml_conv_w, ml_conv_b, ml_norm_w, w_out, norm_ffn_w, ffn_w_up, ffn_conv_w, ffn_conv_b, ffn_w_down, final_norm_w):
    depth, d, _ = ada_w.shape
    bp, seq_p, _ = x_prompt.shape
    bs, seq_s, _ = x_sample.shape
    heads = ML_HEADS
    d_hy = hy_bias.shape[-1]
    d_ml = ml_norm_w.shape[-1]
    dh = d_ml // heads
    assert bs < MOD_ROWS and seq_s % GRID_W == 0

    cvec = jnp.concatenate([c, c_ctx[None, :], jnp.zeros((MOD_ROWS - bs - 1, d), F32)], axis=0)
    mod_all = _ada(cvec, ada_w, ada_b).reshape(depth, MOD_ROWS, 6, d)
    row_ctx = lambda b: bs
    row_lat = lambda b: b

    tabs_p = _dft_tables(seq_p)
    tabs_s = _dft_tables(seq_s)
    zc = jnp.zeros((bp, 2, heads, dh, dh), F32)
    zn = jnp.zeros((bp, 2, heads, dh), F32)
    zm = jnp.zeros((bp, 2, heads), F32)
    hy_filt = (hy_filt_w1, hy_filt_b1, hy_filt_w2, hy_filt_b2, hy_filt_w3, hy_filt_b3, hy_filt_w4, hy_freq)

    xp, xs = x_prompt, x_sample
    new_c, new_n, new_m = [], [], []
    for l in range(depth):
        p = _layer_params(l, d_hy, d_ml, norm_mix_w, w_in, b_gate, hy_conv_w, hy_conv_b, hy_filt,
                          hy_bias, ml_conv_w, ml_conv_b, ml_norm_w, w_out, norm_ffn_w, ffn_w_up,
                          ffn_conv_w, ffn_conv_b, ffn_w_down, final_norm_w)
        final = l == depth - 1
        xp, cl, nl, ml = _layer(xp, mod_all[l], row_ctx, p, tabs_p, zc, zn, zm, 1, final)
        new_c.append(cl)
        new_n.append(nl)
        new_m.append(ml)
        xs, _, _, _ = _layer(xs, mod_all[l], row_lat, p, tabs_s, state_mlstm_C[:, l],
                             state_mlstm_n[:, l], state_mlstm_m[:, l], seq_s // GRID_W, final)
    return (xp, xs, jnp.stack(new_c, axis=1), jnp.stack(new_n, axis=1), jnp.stack(new_m, axis=1))
```

```python
import functools
import math

import numpy as np
import jax
import jax.numpy as jnp
from jax import lax
from jax.experimental import pallas as pl
from jax.experimental.pallas import tpu as pltpu

F32 = jnp.float32
BF16 = jnp.bfloat16
HIGHEST = lax.Precision.HIGHEST

GRID_W = 64
ML_HEADS = 4
CHUNK = 128
HY_BANDS = 16
HY_DECAY_TARGET = 1e-2
HY_FAST_DECAY_PCT = 0.3
HY_SLOW_DECAY_PCT = 1.5
HY_MIN_DECAY = math.log(HY_DECAY_TARGET) / HY_SLOW_DECAY_PCT
HY_MAX_DECAY = math.log(HY_DECAY_TARGET) / HY_FAST_DECAY_PCT
EPS = 1e-6

LANE = 128
HALO_ROWS = 16
VMEM_LIMIT = 56 * 1024 * 1024
MOD_ROWS = 8
DFT_ROW_TILE = 64
FFN_TILE = 256
DFT_M_TILE, DFT_K_TILE = 1024, 512


def _params(*sem):
    return pltpu.CompilerParams(dimension_semantics=sem, vmem_limit_bytes=VMEM_LIMIT)


def _silu(x):
    return x * jax.nn.sigmoid(x)


def _rms(x, w):
    return x * lax.rsqrt(jnp.mean(x * x, axis=-1, keepdims=True) + EPS) * w


def _split_hi_lo(x):
    hi = x.astype(BF16)
    return hi, (x - hi.astype(F32)).astype(BF16)


def _ada_kernel(c_ref, w_ref, b_ref, o_ref):
    s = _silu(c_ref[...])
    o_ref[0] = jnp.dot(s, w_ref[0], precision=HIGHEST, preferred_element_type=F32) + b_ref[0]


def _ada(cvec, ada_w, ada_b):
    depth, d, nmod = ada_w.shape
    tn = 1024
    return pl.pallas_call(
        _ada_kernel,
        grid=(depth, nmod // tn),
        in_specs=[
            pl.BlockSpec((MOD_ROWS, d), lambda l, j: (0, 0)),
            pl.BlockSpec((1, d, tn), lambda l, j: (l, 0, j)),
            pl.BlockSpec((1, 1, tn), lambda l, j: (l, 0, j)),
        ],
        out_specs=pl.BlockSpec((1, MOD_ROWS, tn), lambda l, j: (l, 0, j)),
        out_shape=jax.ShapeDtypeStruct((depth, MOD_ROWS, nmod), F32),
        compiler_params=_params("parallel", "parallel"),
        name="ada_mod",
    )(cvec, ada_w, ada_b.reshape(depth, 1, nmod))


def _inproj_kernel(x_ref, mod_ref, nw_ref, why_ref, wml_ref, wg_ref, bg_ref,
                   uhy_ref, uml_ref, g_ref):
    m = mod_ref[0]
    h = _rms(x_ref[0], nw_ref[...]) * (1.0 + m[1:2]) + m[0:1]
    h_hi, h_lo = _split_hi_lo(h)
    uhy_ref[0] = jnp.dot(h_hi, why_ref[...], preferred_element_type=F32).astype(uhy_ref.dtype)
    uml_ref[0] = jnp.dot(h_hi, wml_ref[...], preferred_element_type=F32).astype(uml_ref.dtype)
    wg = wg_ref[...]
    ng = g_ref.shape[-1]
    a = jnp.dot(h_hi, wg, preferred_element_type=F32)
    b = jnp.dot(h_lo, wg[:, :ng], preferred_element_type=F32)
    g_ref[0] = a[:, :ng] + a[:, ng:] + b + bg_ref[...]


def _inproj(x, mod, mod_row, nw, w_hy, w_ml, w_g, b_g):
    bsz, seq, d = x.shape
    tb = min(seq, 512)
    n_hy, n_ml = w_hy.shape[1], w_ml.shape[1]
    tok = lambda n: pl.BlockSpec((1, tb, n), lambda b, i: (b, i, 0))
    full = lambda a: pl.BlockSpec(a.shape, lambda b, i: (0,) * a.ndim)
    return pl.pallas_call(
        _inproj_kernel,
        grid=(bsz, seq // tb),
        in_specs=[tok(d), pl.BlockSpec((1, 6, d), lambda b, i: (mod_row(b), 0, 0)),
                  full(nw), full(w_hy), full(w_ml), full(w_g), full(b_g)],
        out_specs=[tok(n_hy), tok(n_ml), tok(LANE)],
        out_shape=[jax.ShapeDtypeStruct((bsz, seq, n_hy), BF16),
                   jax.ShapeDtypeStruct((bsz, seq, n_ml), BF16),
                   jax.ShapeDtypeStruct((bsz, seq, LANE), F32)],
        compiler_params=_params("parallel", "parallel"),
        name="inproj",
    )(x, mod, nw, w_hy, w_ml, w_g, b_g)


def _dwconv3(u, before, after, w, b):
    n = u.shape[0]
    row = lax.broadcasted_iota(jnp.int32, u.shape, 0)
    prev = jnp.where(row == 0, before, pltpu.roll(u, 1, 0))
    nxt = jnp.where(row == n - 1, after, pltpu.roll(u, n - 1, 0))
    return prev * w[0:1] + u * w[1:2] + nxt * w[2:3] + b


def _tile_neighbours(prev_ref, next_ref):
    i = pl.program_id(1)
    before = jnp.where(i == 0, 0.0, prev_ref[0].astype(F32)[HALO_ROWS - 1:HALO_ROWS])
    after = jnp.where(i == pl.num_programs(1) - 1, 0.0, next_ref[0].astype(F32)[0:1])
    return before, after


def _row_tile_specs(seq, tb, c, cblk=0):
    per = tb // HALO_ROWS
    last = seq // HALO_ROWS - 1
    main = pl.BlockSpec((1, tb, c), lambda b, i: (b, i, cblk))
    prev = pl.BlockSpec((1, HALO_ROWS, c), lambda b, i: (b, jnp.maximum(i * per - 1, 0), cblk))
    nxt = pl.BlockSpec((1, HALO_ROWS, c), lambda b, i: (b, jnp.minimum((i + 1) * per, last), cblk))
    return main, prev, nxt


def _hy_pre_kernel(u_ref, up_ref, un_ref, w_ref, b_ref, v_ref, x0_ref):
    before, after = _tile_neighbours(up_ref, un_ref)
    y = _dwconv3(u_ref[0].astype(F32), before, after, w_ref[...], b_ref[...])
    c = v_ref.shape[-1]
    x0_ref[0] = y[:, :c].astype(x0_ref.dtype)
    v_ref[0] = (y[:, 2 * c:] * y[:, c:2 * c]).astype(v_ref.dtype)


def _hy_pre(u_hy, conv_w, conv_b):
    bsz, seq, c3 = u_hy.shape
    tb = min(seq, 512)
    main, prev, nxt = _row_tile_specs(seq, tb, c3)
    full = lambda a: pl.BlockSpec(a.shape, lambda b, i: (0,) * a.ndim)
    out = pl.BlockSpec((1, tb, c3 // 3), lambda b, i: (b, i, 0))
    shp = jax.ShapeDtypeStruct((bsz, seq, c3 // 3), BF16)
    return pl.pallas_call(
        _hy_pre_kernel,
        grid=(bsz, seq // tb),
        in_specs=[main, prev, nxt, full(conv_w), full(conv_b)],
        out_specs=[out, out],
        out_shape=[shp, shp],
        compiler_params=_params("parallel", "parallel"),
        name="hyena_pre",
    )(u_hy, u_hy, u_hy, conv_w, conv_b)


def _ml_pre_kernel(u_ref, up_ref, un_ref, w_ref, b_ref, o_ref, *, d_ml, kscale):
    before, after = _tile_neighbours(up_ref, un_ref)
    y = _silu(_dwconv3(u_ref[0].astype(F32), before, after, w_ref[...], b_ref[...]))
    lane = lax.broadcasted_iota(jnp.int32, (1, y.shape[1]), 1)
    o_ref[0] = (y * jnp.where(lane >= d_ml, kscale, 1.0)).astype(o_ref.dtype)


def _ml_pre(u_ml, conv_w, conv_b, d_ml, dh):
    bsz, seq, _ = u_ml.shape
    tb = min(seq, 512)
    main, prev, nxt = _row_tile_specs(seq, tb, 2 * d_ml)
    full = lambda a: pl.BlockSpec(a.shape, lambda b, i: (0,) * a.ndim)
    return pl.pallas_call(
        functools.partial(_ml_pre_kernel, d_ml=d_ml, kscale=dh ** -0.5),
        grid=(bsz, seq // tb),
        in_specs=[main, prev, nxt, full(conv_w), full(conv_b)],
        out_specs=pl.BlockSpec((1, tb, 2 * d_ml), lambda b, i: (b, i, 0)),
        out_shape=jax.ShapeDtypeStruct((bsz, seq, 2 * d_ml), BF16),
        compiler_params=_params("parallel", "parallel"),
        name="mlstm_pre",
    )(u_ml, u_ml, u_ml, conv_w, conv_b)


def _dft_table_kernel(pre_ref, pim_ref, qre_ref, qim_ref, c_ref, s_ref, st_ref):
    pre, pim = pre_ref[0], pim_ref[0]
    qre, qim = qre_ref[...], qim_ref[...]
    c = pre * qre - pim * qim
    s = pre * qim + pim * qre
    grow = lax.broadcasted_iota(jnp.int32, c.shape, 0) + pl.program_id(0) * c.shape[0]
    lane = lax.broadcasted_iota(jnp.int32, c.shape, 1)
    alt_lane = (1 - 2 * (lane & 1)).astype(F32)
    alt_row = (1 - 2 * (grow & 1)).astype(F32)
    c_ref[...] = c.astype(BF16)
    s_ref[...] = jnp.where(grow == 0, alt_lane, s).astype(BF16)
    st_ref[...] = jnp.where(lane == 0, alt_row, s).astype(BF16)


def _dft_tables(seq):
    n = 2 * seq
    tm = DFT_ROW_TILE
    s = jnp.arange(seq, dtype=jnp.int32)[None, :]
    ang = lambda f: (-2.0 * math.pi / n) * ((f * s) % n).astype(F32)
    fp = (jnp.arange(seq // tm, dtype=jnp.int32) * tm)[:, None]
    fq = jnp.arange(tm, dtype=jnp.int32)[:, None]
    ap, aq = ang(fp), ang(fq)
    p3 = lambda a: a.reshape(seq // tm, 1, seq)
    pspec = pl.BlockSpec((1, 1, seq), lambda r: (r, 0, 0))
    qspec = pl.BlockSpec((tm, seq), lambda r: (0, 0))
    out = pl.BlockSpec((tm, seq), lambda r: (r, 0))
    shp = jax.ShapeDtypeStruct((seq, seq), BF16)
    return pl.pallas_call(
        _dft_table_kernel,
        grid=(seq // tm,),
        in_specs=[pspec, pspec, qspec, qspec],
        out_specs=[out, out, out],
        out_shape=[shp, shp, shp],
        compiler_params=_params("parallel"),
        name="dft_tables",
    )(p3(jnp.cos(ap)), p3(jnp.sin(ap)), jnp.cos(aq), jnp.sin(aq))


def _filter_kernel(z_ref, w1_ref, b1_ref, w2_ref, b2_ref, w3_ref, b3_ref, fr_ref,
                   w4_ref, dl_ref, a_ref, d_ref, ny_ref, h_ref):
    def dot3(x, w):
        x_hi, x_lo = _split_hi_lo(x)
        w_hi, w_lo = _split_hi_lo(w)
        mm = functools.partial(jnp.dot, preferred_element_type=F32)
        return mm(x_hi, w_hi) + mm(x_hi, w_lo) + mm(x_lo, w_hi)

    @pl.when(pl.program_id(0) == 0)
    def _():
        fr = fr_ref[...]
        h = jnp.sin(fr * (dot3(z_ref[...], w1_ref[...]) + b1_ref[...]))
        h = jnp.sin(fr * (dot3(h, w2_ref[...]) + b2_ref[...]))
        h_ref[...] = jnp.sin(fr * (dot3(h, w3_ref[...]) + b3_ref[...]))

    c = a_ref.shape[1]
    hfb = dot3(h_ref[...], w4_ref[0])
    decay = jnp.exp(-z_ref[:, 0:1] * dl_ref[...])
    hf = hfb[:, :c] * decay
    hb = hfb[:, c:] * decay
    row = lax.broadcasted_iota(jnp.int32, hf.shape, 0)
    hb = jnp.where(row == 0, 0.0, hb)
    inv = 1.0 / jnp.sum(jnp.abs(hf) + jnp.abs(hb), axis=0, keepdims=True)
    a = (hf + hb) * inv
    a_ref[...] = a.astype(BF16)
    d_ref[...] = ((hf - hb) * inv).astype(BF16)
    ny_ref[...] = jnp.sum(a * (1 - 2 * (row & 1)).astype(F32), axis=0, keepdims=True)


def _filter_taps(seq, filt, d_hy):
    w1, b1, w2, b2, w3, b3, w4, freq = filt
    emb, ffn = w1.shape
    t = jnp.linspace(0.0, 1.0, seq, dtype=F32)[:, None]
    wpos = (2.0 * math.pi / seq) * jnp.arange(seq, dtype=F32)[:, None]
    bands = jnp.linspace(1e-4, HY_BANDS - 1, HY_BANDS, dtype=F32)[None, :]
    z = jnp.concatenate([t, jnp.cos(bands * wpos), -jnp.sin(bands * wpos),
                         jnp.zeros((seq, LANE - emb), F32)], axis=-1)
    w1p = jnp.concatenate([w1, jnp.zeros((LANE - emb, ffn), F32)], axis=0)
    deltas = jnp.abs(jnp.linspace(HY_MIN_DECAY, HY_MAX_DECAY, d_hy, dtype=F32))[None, :]
    row = lambda a: a.reshape(1, -1)
    nblk = d_hy // LANE
    w4t = jnp.concatenate([w4[:, :d_hy].reshape(ffn, nblk, LANE), w4[:, d_hy:].reshape(ffn, nblk, LANE)], axis=2)
    w4t = jnp.transpose(w4t, (1, 0, 2))
    full = lambda a: pl.BlockSpec(a.shape, lambda j: (0,) * a.ndim)
    args = [z, w1p, row(b1), w2, row(b2), w3, row(b3), row(freq)]
    col = pl.BlockSpec((seq, LANE), lambda j: (0, j))
    vec = pl.BlockSpec((1, LANE), lambda j: (0, j))
    return pl.pallas_call(
        _filter_kernel,
        grid=(nblk,),
        in_specs=[full(a) for a in args] + [pl.BlockSpec((1, ffn, 2 * LANE), lambda j: (j, 0, 0)), vec],
        out_specs=[col, col, vec],
        out_shape=[jax.ShapeDtypeStruct((seq, d_hy), BF16),
                   jax.ShapeDtypeStruct((seq, d_hy), BF16),
                   jax.ShapeDtypeStruct((1, d_hy), F32)],
        scratch_shapes=[pltpu.VMEM((seq, ffn), F32)],
        compiler_params=_params("arbitrary"),
        name="hyena_filter",
    )(*args, w4t, deltas)


def _filter_dft_kernel(c_ref, s_ref, a_ref, d_ref, ny_ref, kre_ref, kim_ref, are_ref, aim_ref):
    k = pl.program_id(1)

    @pl.when(k == 0)
    def _():
        are_ref[...] = jnp.zeros_like(are_ref)
        aim_ref[...] = jnp.zeros_like(aim_ref)

    are_ref[...] += jnp.dot(c_ref[...], a_ref[...], preferred_element_type=F32)
    aim_ref[...] += jnp.dot(s_ref[...], d_ref[...], preferred_element_type=F32)

    @pl.when(k == pl.num_programs(1) - 1)
    def _():
        grow = lax.broadcasted_iota(jnp.int32, are_ref.shape, 0) + pl.program_id(0) * are_ref.shape[0]
        kre_ref[...] = are_ref[...]
        kim_ref[...] = jnp.where(grow == 0, ny_ref[...], aim_ref[...])


def _filter_dft(ctab, stab, a, d, ny):
    seq, c = a.shape
    t = min(seq, 1024)
    return pl.pallas_call(
        _filter_dft_kernel,
        grid=(seq // t, seq // t),
        in_specs=[pl.BlockSpec((t, t), lambda m, k: (m, k)),
                  pl.BlockSpec((t, t), lambda m, k: (m, k)),
                  pl.BlockSpec((t, c), lambda m, k: (k, 0)),
                  pl.BlockSpec((t, c), lambda m, k: (k, 0)),
                  pl.BlockSpec((1, c), lambda m, k: (0, 0))],
        out_specs=[pl.BlockSpec((t, c), lambda m, k: (m, 0))] * 2,
        out_shape=[jax.ShapeDtypeStruct((seq, c), F32)] * 2,
        scratch_shapes=[pltpu.VMEM((t, c), F32), pltpu.VMEM((t, c), F32)],
        compiler_params=_params("parallel", "arbitrary"),
        name="hyena_filter_dft",
    )(ctab, stab, a, d, ny)


def _dft_fwd_kernel(c_ref, s_ref, v_ref, kre_ref, kim_ref, yre_ref, yim_ref, are_ref, aim_ref):
    k = pl.program_id(2)
    bt = v_ref.shape[0]

    @pl.when(k == 0)
    def _():
        are_ref[...] = jnp.zeros_like(are_ref)
        aim_ref[...] = jnp.zeros_like(aim_ref)

    for b in range(bt):
        vb = v_ref[b].astype(BF16)
        are_ref[b] += jnp.dot(c_ref[...], vb, preferred_element_type=F32)
        aim_ref[b] += jnp.dot(s_ref[...], vb, preferred_element_type=F32)

    @pl.when(k == pl.num_programs(2) - 1)
    def _():
        kre, kim = kre_ref[...], kim_ref[...]
        grow = lax.broadcasted_iota(jnp.int32, kre.shape, 0) + pl.program_id(1) * kre.shape[0]
        packed = grow == 0
        for b in range(bt):
            xr, xi = are_ref[b], aim_ref[b]
            yre = jnp.where(packed, 0.5 * xr * kre, xr * kre - xi * kim)
            yim = jnp.where(packed, 0.5 * xi * kim, xr * kim + xi * kre)
            yre_ref[b] = yre.astype(BF16)
            yim_ref[b] = yim.astype(BF16)


def _dft_fwd(ctab, stab, v, kre, kim, bt):
    bsz, seq, c = v.shape
    tm, tk = min(seq, DFT_M_TILE), min(seq, DFT_K_TILE)
    tab = pl.BlockSpec((tm, tk), lambda b, m, k: (m, k))
    kspec = pl.BlockSpec((tm, c), lambda b, m, k: (m, 0))
    out = pl.BlockSpec((bt, tm, c), lambda b, m, k: (b, m, 0))
    shp = jax.ShapeDtypeStruct((bsz, seq, c), BF16)
    return pl.pallas_call(
        _dft_fwd_kernel,
        grid=(bsz // bt, seq // tm, seq // tk),
        in_specs=[tab, tab, pl.BlockSpec((bt, tk, c), lambda b, m, k: (b, k, 0)), kspec, kspec],
        out_specs=[out, out],
        out_shape=[shp, shp],
        scratch_shapes=[pltpu.VMEM((bt, tm, c), F32), pltpu.VMEM((bt, tm, c), F32)],
        compiler_params=_params("parallel", "parallel", "arbitrary"),
        name="hyena_dft_fwd",
    )(ctab, stab, v, kre, kim)


def _dft_inv_kernel(c_ref, st_ref, yre_ref, yim_ref, v_ref, x0_ref, hb_ref, o_ref, acc_ref, *, scale):
    k = pl.program_id(2)
    bt = v_ref.shape[0]

    @pl.when(k == 0)
    def _():
        acc_ref[...] = jnp.zeros_like(acc_ref)

    for b in range(bt):
        acc_ref[b] += (jnp.dot(c_ref[...], yre_ref[b], preferred_element_type=F32)
                       + jnp.dot(st_ref[...], yim_ref[b], preferred_element_type=F32))

    @pl.when(k == pl.num_programs(2) - 1)
    def _():
        for b in range(bt):
            v = v_ref[b]
            y = (acc_ref[b] * scale + hb_ref[...] * v.astype(F32)) * x0_ref[b].astype(F32)
            o_ref[b] = y.astype(o_ref.dtype)


def _dft_inv(ctab, sttab, yre, yim, v, x0, hy_bias, bt):
    bsz, seq, c = v.shape
    tm, tk = min(seq, DFT_M_TILE), min(seq, DFT_K_TILE)
    tab = pl.BlockSpec((tm, tk), lambda b, m, k: (m, k))
    yspec = pl.BlockSpec((bt, tk, c), lambda b, m, k: (b, k, 0))
    tok = pl.BlockSpec((bt, tm, c), lambda b, m, k: (b, m, 0))
    return pl.pallas_call(
        functools.partial(_dft_inv_kernel, scale=1.0 / seq),
        grid=(bsz // bt, seq // tm, seq // tk),
        in_specs=[tab, tab, yspec, yspec, tok, tok, pl.BlockSpec((1, c), lambda b, m, k: (0, 0))],
        out_specs=tok,
        out_shape=jax.ShapeDtypeStruct((bsz, seq, c), BF16),
        scratch_shapes=[pltpu.VMEM((bt, tm, c), F32)],
        compiler_params=_params("parallel", "parallel", "arbitrary"),
        name="hyena_dft_inv",
    )(ctab, sttab, yre, yim, v, x0, hy_bias)


def _log_sigmoid(x):
    return jnp.minimum(x, 0.0) - jnp.log1p(jnp.exp(-jnp.abs(x)))


def _mlstm_kernel(*refs, heads, dh, zero_state):
    q_ref, k_ref, v_ref, g_ref, gt_ref = refs[:5]
    h_ref, c_out_ref, n_out_ref, m_out_ref, c_s, n_s, m_s = refs[-7:]
    step = pl.program_id(2)
    fwd = pl.program_id(0) == 0
    bt, t = q_ref.shape[0], q_ref.shape[1]

    @pl.when(step == 0)
    def _():
        if zero_state:
            c_s[...] = jnp.zeros_like(c_s)
            n_s[...] = jnp.zeros_like(n_s)
            m_s[...] = jnp.zeros_like(m_s)
        else:
            c0_ref, n0_ref, m0_ref = refs[5:8]
            c_s[...] = c0_ref[:, 0]
            n_s[...] = n0_ref[:, 0]
            m_s[...] = m0_ref[:, 0]

    row = lax.broadcasted_iota(jnp.int32, (t, t), 0)
    col = lax.broadcasted_iota(jnp.int32, (t, t), 1)
    sgn = jnp.where(fwd, 1, -1)
    seen = sgn * (row - col) >= 0
    seen_b = seen.astype(BF16)
    seen_tb = (sgn * (col - row) >= 0).astype(BF16)
    nt = (((1,), (1,)), ((), ()))
    tn = (((0,), (0,)), ((), ()))

    def split3(x, axis):
        hi = x.astype(BF16).astype(F32)
        r = x - hi
        mid = r.astype(BF16).astype(F32)
        return jnp.concatenate([hi, mid, r - mid], axis=axis).astype(BF16)

    g = g_ref[0, 0]
    gt = gt_ref[0, 0]
    lf = _log_sigmoid(g)
    lf_tot = jnp.sum(lf, axis=0, keepdims=True)
    cs = jnp.dot(seen_b, split3(lf, 1), preferred_element_type=F32)
    nl = lf.shape[1]
    bcol_all = cs[:, :nl] + cs[:, nl:2 * nl] + cs[:, 2 * nl:]
    rs = jnp.dot(split3(_log_sigmoid(gt), 0), seen_tb, preferred_element_type=F32)
    nr = gt.shape[0]
    brow_all = rs[:nr] + rs[nr:2 * nr] + rs[2 * nr:]
    chains = [(b, hd) for b in range(bt) for hd in range(heads)]
    sl = lambda hd: slice(hd * dh, (hd + 1) * dh)
    qs = [q_ref[b, :, sl(hd)] for b, hd in chains]
    ks = [k_ref[b, :, sl(hd)] for b, hd in chains]
    vs = [v_ref[b, :, sl(hd)].astype(BF16) for b, hd in chains]
    qk = [lax.dot_general(q, k, nt, preferred_element_type=F32) for q, k in zip(qs, ks)]
    each = lambda f, *ls: [f(*a) for a in zip(*ls)]
    fcols = [b * 2 * heads + heads + hd for b, hd in chains]
    bc = [bcol_all[:, c:c + 1] for c in fcols]
    rowt = [gt[c - heads:c - heads + 1, :] - brow_all[c:c + 1, :] for c in fcols]
    ic = [g[:, c - heads:c - heads + 1] for c in fcols]
    b_end = [lf_tot[:, c:c + 1] for c in fcols]
    m_prev = [m_s[b, hd:hd + 1, 0:1] for b, hd in chains]
    ct_prev = [c_s[b, hd] for b, hd in chains]
    n_prev = [n_s[b, hd:hd + 1, :] for b, hd in chains]
    dmat = each(lambda x, y: jnp.where(seen, x + y, -jnp.inf), bc, rowt)
    inter = each(lambda x, m: x + m, bc, m_prev)
    dmax = each(lambda d: jnp.max(d, axis=1, keepdims=True), dmat)
    m_t = each(jnp.maximum, inter, dmax)
    s = each(lambda r, d, m: r * jnp.exp(d - m), qk, dmat, m_t)
    w_inter = each(lambda i, m: jnp.exp(i - m), inter, m_t)
    qf = each(lambda q: q.astype(F32), qs)
    lhs = each(lambda x, q, w: jnp.concatenate([x.astype(BF16), (q * w).astype(BF16)], axis=1),
               s, qf, w_inter)
    rhs = each(lambda v, c: jnp.concatenate([v, c.astype(BF16)], axis=0), vs, ct_prev)
    nums = each(lambda l, r: jnp.dot(l, r, preferred_element_type=F32), lhs, rhs)
    qn = each(lambda q, n: jnp.sum(q * n, axis=1, keepdims=True), qf, n_prev)
    ssum = each(lambda x: jnp.sum(x, axis=1, keepdims=True), s)
    den = each(lambda a, w, x: a + w * x, ssum, w_inter, qn)
    w_log = each(lambda e, x, i: e - x + i, b_end, bc, ic)
    wmax = each(lambda w: jnp.max(w, axis=0, keepdims=True), w_log)
    m_new = each(lambda e, m, w: jnp.maximum(e + m, w), b_end, m_prev, wmax)
    w_s = each(lambda w, m: jnp.exp(w - m), w_log, m_new)
    w_c = each(lambda e, m, mn: jnp.exp(e + m - mn), b_end, m_prev, m_new)
    kw = each(lambda k, w: k.astype(F32) * w, ks, w_s)
    upds = each(lambda x, v: lax.dot_general(x.astype(BF16), v, tn, preferred_element_type=F32), kw, vs)
    ksum = each(lambda x: jnp.sum(x, axis=0, keepdims=True), kw)
    for i, (b, hd) in enumerate(chains):
        h_ref[0, b, :, sl(hd)] = nums[i] / jnp.maximum(jnp.abs(den[i]), jnp.exp(-m_t[i]))
        c_s[b, hd] = w_c[i] * ct_prev[i] + upds[i]
        n_s[b, hd:hd + 1, :] = w_c[i] * n_prev[i] + ksum[i]
        m_s[b, hd:hd + 1, :] = jnp.broadcast_to(m_new[i], (1, dh))

    @pl.when(step == pl.num_programs(2) - 1)
    def _():
        c_out_ref[:, 0] = c_s[...]
        n_out_ref[:, 0] = n_s[...]
        m_out_ref[:, 0] = m_s[...]


def _mlstm(qk, u_ml, g, c0, n0, m0, d_ml):
    bsz, seq, _ = qk.shape
    heads = ML_HEADS
    dh = d_ml // heads
    nc = seq // CHUNK
    bt = math.gcd(bsz, 4)
    nbb = bsz // bt
    gi = g[..., :4 * heads].reshape(nbb, bt, seq, 2, 2 * heads)
    g2 = jnp.transpose(gi, (3, 0, 2, 1, 4)).reshape(2, nbb, seq, bt * 2 * heads)
    gt2 = jnp.swapaxes(g2, 2, 3)
    g2 = jnp.pad(g2, ((0, 0), (0, 0), (0, 0), (0, LANE - bt * 2 * heads)))
    chunk = lambda d, c: c + d * (nc - 1 - 2 * c)
    tok = lambda j: pl.BlockSpec((bt, CHUNK, d_ml), lambda d, b, c: (b, chunk(d, c), j))
    st = lambda *tail: pl.BlockSpec((bt, 1, heads) + tail, lambda d, b, c: (b, d, 0) + (0,) * len(tail))
    zero_state = c0 is None
    states = [] if zero_state else [jnp.swapaxes(c0, -1, -2), n0,
                                    jnp.broadcast_to(m0[..., None], m0.shape + (dh,))]
    h, ct, n, m = pl.pallas_call(
        functools.partial(_mlstm_kernel, heads=heads, dh=dh, zero_state=zero_state),
        grid=(2, nbb, nc),
        in_specs=[tok(0), tok(1), tok(2),
                  pl.BlockSpec((1, 1, CHUNK, LANE), lambda d, b, c: (d, b, chunk(d, c), 0)),
                  pl.BlockSpec((1, 1, bt * 2 * heads, CHUNK), lambda d, b, c: (d, b, 0, chunk(d, c)))]
                 + ([] if zero_state else [st(dh, dh), st(dh), st(dh)]),
        out_specs=[pl.BlockSpec((1, bt, CHUNK, d_ml), lambda d, b, c: (d, b, chunk(d, c), 0)),
                   st(dh, dh), st(dh), st(dh)],
        out_shape=[jax.ShapeDtypeStruct((2, bsz, seq, d_ml), F32),
                   jax.ShapeDtypeStruct((bsz, 2, heads, dh, dh), F32),
                   jax.ShapeDtypeStruct((bsz, 2, heads, dh), F32),
                   jax.ShapeDtypeStruct((bsz, 2, heads, dh), F32)],
        scratch_shapes=[pltpu.VMEM((bt, heads, dh, dh), F32), pltpu.VMEM((bt, heads, dh), F32),
                        pltpu.VMEM((bt, heads, dh), F32)],
        compiler_params=_params("parallel", "parallel", "arbitrary"),
        name="mlstm_scan",
    )(qk, qk, u_ml, g2, gt2, *states)
    return h, jnp.swapaxes(ct, -1, -2), n, m[..., 0]


def _mix_out_kernel(yhy_ref, hf_ref, hb_ref, o_ref, x_ref, mod_ref, mlw_ref, wout_ref, fnw_ref,
                    x1_ref, h2_ref, *, heads, dh):
    m = mod_ref[0]
    h = hf_ref[0, 0] + hb_ref[0, 0]
    o = o_ref[0].astype(F32)
    mlw = mlw_ref[...]
    parts = [yhy_ref[0].astype(BF16)]
    for hd in range(heads):
        sl = slice(hd * dh, (hd + 1) * dh)
        parts.append((_rms(h[:, sl], mlw[:, sl]) * jax.nn.sigmoid(o[:, sl])).astype(BF16))
    y = jnp.concatenate(parts, axis=-1)
    x1 = x_ref[0] + m[2:3] * jnp.dot(y, wout_ref[...], preferred_element_type=F32)
    x1_ref[0] = x1
    h2_ref[0] = (_rms(x1, fnw_ref[...]) * (1.0 + m[4:5]) + m[3:4]).astype(BF16)


def _mix_out(y_hy, h2dir, u_ml, x, mod, mod_row, ml_norm_w, w_out, norm_ffn_w):
    bsz, seq, d = x.shape
    d_ml = h2dir.shape[-1]
    tb = min(seq, 512)
    tok = lambda n, j=0: pl.BlockSpec((1, tb, n), lambda b, i: (b, i, j))
    hdir = lambda dr: pl.BlockSpec((1, 1, tb, d_ml), lambda b, i: (dr, b, i, 0))
    full = lambda a: pl.BlockSpec(a.shape, lambda b, i: (0,) * a.ndim)
    return pl.pallas_call(
        functools.partial(_mix_out_kernel, heads=ML_HEADS, dh=d_ml // ML_HEADS),
        grid=(bsz, seq // tb),
        in_specs=[tok(y_hy.shape[-1]), hdir(0), hdir(1), tok(d_ml, 3), tok(d),
                  pl.BlockSpec((1, 6, d), lambda b, i: (mod_row(b), 0, 0)),
                  full(ml_norm_w), full(w_out), full(norm_ffn_w)],
        out_specs=[tok(d), tok(d)],
        out_shape=[jax.ShapeDtypeStruct((bsz, seq, d), F32),
                   jax.ShapeDtypeStruct((bsz, seq, d), BF16)],
        compiler_params=_params("parallel", "parallel"),
        name="mixer_out",
    )(y_hy, h2dir, h2dir, u_ml, x, mod, ml_norm_w, w_out, norm_ffn_w)


def _ffn_kernel(*refs, width, halo, final, nj):
    if halo:
        (top_ref, main_ref, bot_ref, x1_ref, mod_ref, wup_ref, cw_ref, cb_ref, wd_ref, fw_ref,
         o_ref, acc_ref) = refs
    else:
        (main_ref, x1_ref, mod_ref, wup_ref, cw_ref, cb_ref, wd_ref, fw_ref, o_ref, acc_ref) = refs
    d = main_ref.shape[-1]
    hm = main_ref[...].reshape(-1, d)
    m_rows = hm.shape[0]
    acc_ref[...] = jnp.zeros_like(acc_ref)

    col = lax.broadcasted_iota(jnp.int32, (m_rows, 1), 0) % width
    not_first = (col != 0).astype(F32)
    not_last = (col != width - 1).astype(F32)
    if halo:
        r = pl.program_id(1)
        top_ok = jnp.where(r > 0, 1.0, 0.0)
        bot_ok = jnp.where(r < pl.num_programs(1) - 1, 1.0, 0.0)
        ht, hb = top_ref[0], bot_ref[0]

    def conv(w, cw, cb):
        um = jnp.dot(hm, w, preferred_element_type=F32)
        slabs = [(1, um)]
        if halo:
            ut = jnp.dot(ht, w, preferred_element_type=F32) * top_ok
            ub = jnp.dot(hb, w, preferred_element_type=F32) * bot_ok
            full = jnp.concatenate([ut, um, ub], axis=0)
            slabs = [(kr, full[kr * width:kr * width + m_rows]) for kr in range(3)]
        left = sum(s * cw[3 * kr:3 * kr + 1] for kr, s in slabs)
        mid = sum(s * cw[3 * kr + 1:3 * kr + 2] for kr, s in slabs)
        right = sum(s * cw[3 * kr + 2:3 * kr + 3] for kr, s in slabs)
        return (mid + not_first * pltpu.roll(left, 1, 0)
                + not_last * pltpu.roll(right, m_rows - 1, 0) + cb)

    def tile(j, carry):
        a = conv(wup_ref[j], cw_ref[j], cb_ref[j])
        val = conv(wup_ref[nj + j], cw_ref[nj + j], cb_ref[nj + j])
        act = (_silu(a) * val).astype(BF16)
        acc_ref[...] += jnp.dot(act, wd_ref[j], preferred_element_type=F32)
        return carry

    lax.fori_loop(0, nj, tile, 0)
    m = mod_ref[0]
    x2 = x1_ref[...].reshape(-1, d) + m[5:6] * acc_ref[...]
    if final:
        x2 = _rms(x2, fw_ref[...])
    o_ref[...] = x2.reshape(o_ref.shape)


def _ffn(h2, x1, mod, mod_row, w_up, conv_w, conv_b, w_down, final_w, rows, final):
    bsz, seq, d = x1.shape
    nj = w_down.shape[0]
    width = seq // rows
    halo = rows > 1
    if halo:
        rb = 16
        nb, m_tok, nr = 1, rb * width, rows // rb
    else:
        nb, m_tok, nr = math.gcd(bsz, 4), seq, 1
    tok = pl.BlockSpec((nb, m_tok, d), lambda b, r: (b, r, 0))
    whole = lambda a: pl.BlockSpec(a.shape, lambda b, r: (0,) * a.ndim, pipeline_mode=pl.Buffered(1))
    in_specs, args = [], []
    if halo:
        in_specs += [pl.BlockSpec((1, width, d), lambda b, r: (b, jnp.maximum(r * rb - 1, 0), 0)),
                     tok,
                     pl.BlockSpec((1, width, d), lambda b, r: (b, jnp.minimum((r + 1) * rb, rows - 1), 0))]
        args += [h2, h2, h2]
    else:
        in_specs += [tok]
        args += [h2]
    in_specs += [tok, pl.BlockSpec((1, 6, d), lambda b, r: (mod_row(b), 0, 0)),
                 whole(w_up), whole(conv_w), whole(conv_b), whole(w_down), whole(final_w)]
    args += [x1, mod, w_up, conv_w, conv_b, w_down, final_w]
    return pl.pallas_call(
        functools.partial(_ffn_kernel, width=width, halo=halo, final=final, nj=nj),
        grid=(bsz // nb, nr),
        in_specs=in_specs,
        out_specs=tok,
        out_shape=jax.ShapeDtypeStruct((bsz, seq, d), F32),
        scratch_shapes=[pltpu.VMEM((nb * m_tok, d), F32)],
        compiler_params=_params("parallel", "parallel"),
        name="conv_ffn",
    )(*args)


def _layer(x, mod, mod_row, p, tabs, c0, n0, m0, rows, final):
    bsz, seq, d = x.shape
    d_hy, d_ml = p["d_hy"], p["d_ml"]
    heads = ML_HEADS
    ctab, stab, sttab = tabs
    u_hy, u_ml, g = _inproj(x, mod, mod_row, p["norm_mix_w"], p["w_hy"], p["w_ml"], p["w_g"], p["b_g"])
    v, x0 = _hy_pre(u_hy, p["hy_conv_w"], p["hy_conv_b"])
    a, dd, ny = _filter_taps(seq, p["hy_filt"], d_hy)
    kre, kim = _filter_dft(ctab, stab, a, dd, ny)
    bt = math.gcd(bsz, 4)
    yre, yim = _dft_fwd(ctab, stab, v, kre, kim, bt)
    y_hy = _dft_inv(ctab, sttab, yre, yim, v, x0, p["hy_bias"], bt)
    qk = _ml_pre(u_ml, p["ml_conv_w"], p["ml_conv_b"], d_ml, d_ml // heads)
    h2dir, c_new, n_new, m_new = _mlstm(qk, u_ml, g, c0, n0, m0, d_ml)
    x1, h2 = _mix_out(y_hy, h2dir, u_ml, x, mod, mod_row, p["ml_norm_w"], p["w_out"], p["norm_ffn_w"])
    x2 = _ffn(h2, x1, mod, mod_row, p["ffn_w_up"], p["ffn_conv_w"], p["ffn_conv_b"],
              p["ffn_w_down"], p["final_norm_w"], rows, final)
    return x2, c_new, n_new, m_new


def _layer_params(l, d_hy, d_ml, norm_mix_w, w_in, b_gate, hy_conv_w, hy_conv_b, hy_filt, hy_bias,
                  ml_conv_w, ml_conv_b, ml_norm_w, w_out, norm_ffn_w, ffn_w_up, ffn_conv_w,
                  ffn_conv_b, ffn_w_down, final_norm_w):
    w = w_in[l]
    n_gates = b_gate.shape[-1]
    d_ff = ffn_w_down.shape[1]
    wg = jnp.pad(w[:, 3 * d_hy + 4 * d_ml:], ((0, 0), (0, LANE - n_gates)))
    nj = d_ff // FFN_TILE
    tiles = lambda a: jnp.transpose(a.reshape(a.shape[0], 2 * nj, FFN_TILE), (1, 0, 2))
    return {
        "d_hy": d_hy, "d_ml": d_ml,
        "norm_mix_w": norm_mix_w[l][None, :],
        "w_hy": w[:, :3 * d_hy].astype(BF16),
        "w_ml": w[:, 3 * d_hy:3 * d_hy + 4 * d_ml].astype(BF16),
        "w_g": jnp.concatenate(_split_hi_lo(wg), axis=1),
        "b_g": jnp.pad(b_gate[l], (0, LANE - n_gates))[None, :],
        "hy_conv_w": hy_conv_w[l], "hy_conv_b": hy_conv_b[l][None, :],
        "hy_filt": tuple(a[l] for a in hy_filt),
        "hy_bias": hy_bias[l][None, :],
        "ml_conv_w": ml_conv_w[l], "ml_conv_b": ml_conv_b[l][None, :],
        "ml_norm_w": ml_norm_w[l][None, :],
        "w_out": w_out[l].astype(BF16),
        "norm_ffn_w": norm_ffn_w[l][None, :],
        "ffn_w_up": tiles(ffn_w_up[l]).astype(BF16),
        "ffn_conv_w": tiles(ffn_conv_w[l].reshape(9, 2 * d_ff)),
        "ffn_conv_b": ffn_conv_b[l].reshape(2 * nj, 1, FFN_TILE),
        "ffn_w_down": ffn_w_down[l].astype(BF16).reshape(nj, FFN_TILE, -1),
        "final_norm_w": final_norm_w[None, :],
    }


def kernel(x_prompt, x_sample, state_mlstm_C, state_mlstm_n, state_mlstm_m, c, c_ctx, ada_w, ada_b, norm_mix_w, w_in, b_gate, hy_conv_w, hy_conv_b, hy_filt_w1, hy_filt_b1, hy_filt_w2, hy_filt_b2, hy_filt_w3, hy_filt_b3, hy_filt_w4, hy_freq, hy_bias, ml_conv_w, ml_conv_b, ml_norm_w, w_out, norm_ffn_w, ffn_w_up, ffn_conv_w, ffn_conv_b, ffn_w_down, final_norm_w):
    depth, d, _ = ada_w.shape
    bp, seq_p, _ = x_prompt.shape
    bs, seq_s, _ = x_sample.shape
    heads = ML_HEADS
    d_hy = hy_bias.shape[-1]
    d_ml = ml_norm_w.shape[-1]
    dh = d_ml // heads
    assert bs < MOD_ROWS and seq_s % GRID_W == 0

    cvec = jnp.concatenate([c, c_ctx[None, :], jnp.zeros((MOD_ROWS - bs - 1, d), F32)], axis=0)
    mod_all = _ada(cvec, ada_w, ada_b).reshape(depth, MOD_ROWS, 6, d)
    row_ctx = lambda b: bs
    row_lat = lambda b: b

    tabs_p = _dft_tables(seq_p)
    tabs_s = _dft_tables(seq_s)
    hy_filt = (hy_filt_w1, hy_filt_b1, hy_filt_w2, hy_filt_b2, hy_filt_w3, hy_filt_b3, hy_filt_w4, hy_freq)

    xp, xs = x_prompt, x_sample
    new_c, new_n, new_m = [], [], []
    for l in range(depth):
        p = _layer_params(l, d_hy, d_ml, norm_mix_w, w_in, b_gate, hy_conv_w, hy_conv_b, hy_filt,
                          hy_bias, ml_conv_w, ml_conv_b, ml_norm_w, w_out, norm_ffn_w, ffn_w_up,
                          ffn_conv_w, ffn_conv_b, ffn_w_down, final_norm_w)
        final = l == depth - 1
        xp, cl, nl, ml = _layer(xp, mod_all[l], row_ctx, p, tabs_p, None, None, None, 1, final)
        new_c.append(cl)
        new_n.append(nl)
        new_m.append(ml)
        xs, _, _, _ = _layer(xs, mod_all[l], row_lat, p, tabs_s, state_mlstm_C[:, l],
                             state_mlstm_n[:, l], state_mlstm_m[:, l], seq_s // GRID_W, final)
    return (xp, xs, jnp.stack(new_c, axis=1), jnp.stack(new_n, axis=1), jnp.stack(new_m, axis=1))
```

```python
import functools
import math

import numpy as np
import jax
import jax.numpy as jnp
from jax import lax
from jax.experimental import pallas as pl
from jax.experimental.pallas import tpu as pltpu

F32 = jnp.float32
BF16 = jnp.bfloat16
HIGHEST = lax.Precision.HIGHEST

GRID_W = 64
ML_HEADS = 4
CHUNK = 128
HY_BANDS = 16
HY_DECAY_TARGET = 1e-2
HY_FAST_DECAY_PCT = 0.3
HY_SLOW_DECAY_PCT = 1.5
HY_MIN_DECAY = math.log(HY_DECAY_TARGET) / HY_SLOW_DECAY_PCT
HY_MAX_DECAY = math.log(HY_DECAY_TARGET) / HY_FAST_DECAY_PCT
EPS = 1e-6

LANE = 128
HALO_ROWS = 16
VMEM_LIMIT = 56 * 1024 * 1024
MOD_ROWS = 8
DFT_ROW_TILE = 64
FFN_TILE = 256
DFT_M_TILE, DFT_K_TILE = 1024, 512


def _params(*sem):
    return pltpu.CompilerParams(dimension_semantics=sem, vmem_limit_bytes=VMEM_LIMIT)


def _silu(x):
    return x * jax.nn.sigmoid(x)


def _rms(x, w):
    return x * lax.rsqrt(jnp.mean(x * x, axis=-1, keepdims=True) + EPS) * w


def _split_hi_lo(x):
    hi = x.astype(BF16)
    return hi, (x - hi.astype(F32)).astype(BF16)


def _ada_kernel(c_ref, w_ref, b_ref, o_ref):
    s = _silu(c_ref[...])
    o_ref[0] = jnp.dot(s, w_ref[0], precision=HIGHEST, preferred_element_type=F32) + b_ref[0]


def _ada(cvec, ada_w, ada_b):
    depth, d, nmod = ada_w.shape
    tn = 1024
    return pl.pallas_call(
        _ada_kernel,
        grid=(depth, nmod // tn),
        in_specs=[
            pl.BlockSpec((MOD_ROWS, d), lambda l, j: (0, 0)),
            pl.BlockSpec((1, d, tn), lambda l, j: (l, 0, j)),
            pl.BlockSpec((1, 1, tn), lambda l, j: (l, 0, j)),
        ],
        out_specs=pl.BlockSpec((1, MOD_ROWS, tn), lambda l, j: (l, 0, j)),
        out_shape=jax.ShapeDtypeStruct((depth, MOD_ROWS, nmod), F32),
        compiler_params=_params("parallel", "parallel"),
        name="ada_mod",
    )(cvec, ada_w, ada_b.reshape(depth, 1, nmod))


def _inproj_kernel(x_ref, mod_ref, nw_ref, why_ref, wml_ref, wg_ref, bg_ref,
                   uhy_ref, uml_ref, g_ref):
    m = mod_ref[0]
    h = _rms(x_ref[0], nw_ref[...]) * (1.0 + m[1:2]) + m[0:1]
    h_hi, h_lo = _split_hi_lo(h)
    uhy_ref[0] = jnp.dot(h_hi, why_ref[...], preferred_element_type=F32).astype(uhy_ref.dtype)
    uml_ref[0] = jnp.dot(h_hi, wml_ref[...], preferred_element_type=F32).astype(uml_ref.dtype)
    wg = wg_ref[...]
    ng = g_ref.shape[-1]
    a = jnp.dot(h_hi, wg, preferred_element_type=F32)
    b = jnp.dot(h_lo, wg[:, :ng], preferred_element_type=F32)
    g_ref[0] = a[:, :ng] + a[:, ng:] + b + bg_ref[...]


def _inproj(x, mod, mod_row, nw, w_hy, w_ml, w_g, b_g):
    bsz, seq, d = x.shape
    tb = min(seq, 512)
    n_hy, n_ml = w_hy.shape[1], w_ml.shape[1]
    tok = lambda n: pl.BlockSpec((1, tb, n), lambda b, i: (b, i, 0))
    full = lambda a: pl.BlockSpec(a.shape, lambda b, i: (0,) * a.ndim)
    return pl.pallas_call(
        _inproj_kernel,
        grid=(bsz, seq // tb),
        in_specs=[tok(d), pl.BlockSpec((1, 6, d), lambda b, i: (mod_row(b), 0, 0)),
                  full(nw), full(w_hy), full(w_ml), full(w_g), full(b_g)],
        out_specs=[tok(n_hy), tok(n_ml), tok(LANE)],
        out_shape=[jax.ShapeDtypeStruct((bsz, seq, n_hy), BF16),
                   jax.ShapeDtypeStruct((bsz, seq, n_ml), BF16),
                   jax.ShapeDtypeStruct((bsz, seq, LANE), F32)],
        compiler_params=_params("parallel", "parallel"),
        name="inproj",
    )(x, mod, nw, w_hy, w_ml, w_g, b_g)


def _dwconv3(u, before, after, w, b):
    n = u.shape[0]
    row = lax.broadcasted_iota(jnp.int32, u.shape, 0)
    prev = jnp.where(row == 0, before, pltpu.roll(u, 1, 0))
    nxt = jnp.where(row == n - 1, after, pltpu.roll(u, n - 1, 0))
    return prev * w[0:1] + u * w[1:2] + nxt * w[2:3] + b


def _tile_neighbours(prev_ref, next_ref):
    i = pl.program_id(1)
    before = jnp.where(i == 0, 0.0, prev_ref[0].astype(F32)[HALO_ROWS - 1:HALO_ROWS])
    after = jnp.where(i == pl.num_programs(1) - 1, 0.0, next_ref[0].astype(F32)[0:1])
    return before, after


def _row_tile_specs(seq, tb, c, cblk=0):
    per = tb // HALO_ROWS
    last = seq // HALO_ROWS - 1
    main = pl.BlockSpec((1, tb, c), lambda b, i: (b, i, cblk))
    prev = pl.BlockSpec((1, HALO_ROWS, c), lambda b, i: (b, jnp.maximum(i * per - 1, 0), cblk))
    nxt = pl.BlockSpec((1, HALO_ROWS, c), lambda b, i: (b, jnp.minimum((i + 1) * per, last), cblk))
    return main, prev, nxt


def _hy_pre_kernel(u_ref, up_ref, un_ref, w_ref, b_ref, v_ref, x0_ref):
    before, after = _tile_neighbours(up_ref, un_ref)
    y = _dwconv3(u_ref[0].astype(F32), before, after, w_ref[...], b_ref[...])
    c = v_ref.shape[-1]
    x0_ref[0] = y[:, :c].astype(x0_ref.dtype)
    v_ref[0] = (y[:, 2 * c:] * y[:, c:2 * c]).astype(v_ref.dtype)


def _hy_pre(u_hy, conv_w, conv_b):
    bsz, seq, c3 = u_hy.shape
    tb = min(seq, 512)
    main, prev, nxt = _row_tile_specs(seq, tb, c3)
    full = lambda a: pl.BlockSpec(a.shape, lambda b, i: (0,) * a.ndim)
    out = pl.BlockSpec((1, tb, c3 // 3), lambda b, i: (b, i, 0))
    shp = jax.ShapeDtypeStruct((bsz, seq, c3 // 3), BF16)
    return pl.pallas_call(
        _hy_pre_kernel,
        grid=(bsz, seq // tb),
        in_specs=[main, prev, nxt, full(conv_w), full(conv_b)],
        out_specs=[out, out],
        out_shape=[shp, shp],
        compiler_params=_params("parallel", "parallel"),
        name="hyena_pre",
    )(u_hy, u_hy, u_hy, conv_w, conv_b)


def _ml_pre_kernel(u_ref, up_ref, un_ref, w_ref, b_ref, o_ref, *, d_ml, kscale):
    before, after = _tile_neighbours(up_ref, un_ref)
    y = _silu(_dwconv3(u_ref[0].astype(F32), before, after, w_ref[...], b_ref[...]))
    lane = lax.broadcasted_iota(jnp.int32, (1, y.shape[1]), 1)
    o_ref[0] = (y * jnp.where(lane >= d_ml, kscale, 1.0)).astype(o_ref.dtype)


def _ml_pre(u_ml, conv_w, conv_b, d_ml, dh):
    bsz, seq, _ = u_ml.shape
    tb = min(seq, 512)
    main, prev, nxt = _row_tile_specs(seq, tb, 2 * d_ml)
    full = lambda a: pl.BlockSpec(a.shape, lambda b, i: (0,) * a.ndim)
    return pl.pallas_call(
        functools.partial(_ml_pre_kernel, d_ml=d_ml, kscale=dh ** -0.5),
        grid=(bsz, seq // tb),
        in_specs=[main, prev, nxt, full(conv_w), full(conv_b)],
        out_specs=pl.BlockSpec((1, tb, 2 * d_ml), lambda b, i: (b, i, 0)),
        out_shape=jax.ShapeDtypeStruct((bsz, seq, 2 * d_ml), BF16),
        compiler_params=_params("parallel", "parallel"),
        name="mlstm_pre",
    )(u_ml, u_ml, u_ml, conv_w, conv_b)


def _dft_table_kernel(pre_ref, pim_ref, qre_ref, qim_ref, c_ref, s_ref, st_ref):
    pre, pim = pre_ref[0], pim_ref[0]
    qre, qim = qre_ref[...], qim_ref[...]
    c = pre * qre - pim * qim
    s = pre * qim + pim * qre
    grow = lax.broadcasted_iota(jnp.int32, c.shape, 0) + pl.program_id(0) * c.shape[0]
    lane = lax.broadcasted_iota(jnp.int32, c.shape, 1)
    alt_lane = (1 - 2 * (lane & 1)).astype(F32)
    alt_row = (1 - 2 * (grow & 1)).astype(F32)
    c_ref[...] = c.astype(BF16)
    s_ref[...] = jnp.where(grow == 0, alt_lane, s).astype(BF16)
    st_ref[...] = jnp.where(lane == 0, alt_row, s).astype(BF16)


def _dft_tables(seq):
    n = 2 * seq
    tm = DFT_ROW_TILE
    s = jnp.arange(seq, dtype=jnp.int32)[None, :]
    ang = lambda f: (-2.0 * math.pi / n) * ((f * s) % n).astype(F32)
    fp = (jnp.arange(seq // tm, dtype=jnp.int32) * tm)[:, None]
    fq = jnp.arange(tm, dtype=jnp.int32)[:, None]
    ap, aq = ang(fp), ang(fq)
    p3 = lambda a: a.reshape(seq // tm, 1, seq)
    pspec = pl.BlockSpec((1, 1, seq), lambda r: (r, 0, 0))
    qspec = pl.BlockSpec((tm, seq), lambda r: (0, 0))
    out = pl.BlockSpec((tm, seq), lambda r: (r, 0))
    shp = jax.ShapeDtypeStruct((seq, seq), BF16)
    return pl.pallas_call(
        _dft_table_kernel,
        grid=(seq // tm,),
        in_specs=[pspec, pspec, qspec, qspec],
        out_specs=[out, out, out],
        out_shape=[shp, shp, shp],
        compiler_params=_params("parallel"),
        name="dft_tables",
    )(p3(jnp.cos(ap)), p3(jnp.sin(ap)), jnp.cos(aq), jnp.sin(aq))


def _filter_kernel(z_ref, w1_ref, b1_ref, w2_ref, b2_ref, w3_ref, b3_ref, fr_ref,
                   w4_ref, dl_ref, a_ref, d_ref, ny_ref, h_ref):
    def dot3(x, w):
        x_hi, x_lo = _split_hi_lo(x)
        w_hi, w_lo = _split_hi_lo(w)
        mm = functools.partial(jnp.dot, preferred_element_type=F32)
        return mm(x_hi, w_hi) + mm(x_hi, w_lo) + mm(x_lo, w_hi)

    @pl.when(pl.program_id(0) == 0)
    def _():
        fr = fr_ref[...]
        h = jnp.sin(fr * (dot3(z_ref[...], w1_ref[...]) + b1_ref[...]))
        h = jnp.sin(fr * (dot3(h, w2_ref[...]) + b2_ref[...]))
        h_ref[...] = jnp.sin(fr * (dot3(h, w3_ref[...]) + b3_ref[...]))

    c = a_ref.shape[1]
    hfb = dot3(h_ref[...], w4_ref[0])
    decay = jnp.exp(-z_ref[:, 0:1] * dl_ref[...])
    hf = hfb[:, :c] * decay
    hb = hfb[:, c:] * decay
    row = lax.broadcasted_iota(jnp.int32, hf.shape, 0)
    hb = jnp.where(row == 0, 0.0, hb)
    inv = 1.0 / jnp.sum(jnp.abs(hf) + jnp.abs(hb), axis=0, keepdims=True)
    a = (hf + hb) * inv
    a_ref[...] = a.astype(BF16)
    d_ref[...] = ((hf - hb) * inv).astype(BF16)
    ny_ref[...] = jnp.sum(a * (1 - 2 * (row & 1)).astype(F32), axis=0, keepdims=True)


def _filter_taps(seq, filt, d_hy):
    w1, b1, w2, b2, w3, b3, w4, freq = filt
    emb, ffn = w1.shape
    t = jnp.linspace(0.0, 1.0, seq, dtype=F32)[:, None]
    wpos = (2.0 * math.pi / seq) * jnp.arange(seq, dtype=F32)[:, None]
    bands = jnp.linspace(1e-4, HY_BANDS - 1, HY_BANDS, dtype=F32)[None, :]
    z = jnp.concatenate([t, jnp.cos(bands * wpos), -jnp.sin(bands * wpos),
                         jnp.zeros((seq, LANE - emb), F32)], axis=-1)
    w1p = jnp.concatenate([w1, jnp.zeros((LANE - emb, ffn), F32)], axis=0)
    deltas = jnp.abs(jnp.linspace(HY_MIN_DECAY, HY_MAX_DECAY, d_hy, dtype=F32))[None, :]
    row = lambda a: a.reshape(1, -1)
    nblk = d_hy // LANE
    w4t = jnp.concatenate([w4[:, :d_hy].reshape(ffn, nblk, LANE), w4[:, d_hy:].reshape(ffn, nblk, LANE)], axis=2)
    w4t = jnp.transpose(w4t, (1, 0, 2))
    full = lambda a: pl.BlockSpec(a.shape, lambda j: (0,) * a.ndim)
    args = [z, w1p, row(b1), w2, row(b2), w3, row(b3), row(freq)]
    col = pl.BlockSpec((seq, LANE), lambda j: (0, j))
    vec = pl.BlockSpec((1, LANE), lambda j: (0, j))
    return pl.pallas_call(
        _filter_kernel,
        grid=(nblk,),
        in_specs=[full(a) for a in args] + [pl.BlockSpec((1, ffn, 2 * LANE), lambda j: (j, 0, 0)), vec],
        out_specs=[col, col, vec],
        out_shape=[jax.ShapeDtypeStruct((seq, d_hy), BF16),
                   jax.ShapeDtypeStruct((seq, d_hy), BF16),
                   jax.ShapeDtypeStruct((1, d_hy), F32)],
        scratch_shapes=[pltpu.VMEM((seq, ffn), F32)],
        compiler_params=_params("arbitrary"),
        name="hyena_filter",
    )(*args, w4t, deltas)


def _filter_dft_kernel(c_ref, s_ref, a_ref, d_ref, ny_ref, kre_ref, kim_ref, are_ref, aim_ref):
    k = pl.program_id(1)

    @pl.when(k == 0)
    def _():
        are_ref[...] = jnp.zeros_like(are_ref)
        aim_ref[...] = jnp.zeros_like(aim_ref)

    are_ref[...] += jnp.dot(c_ref[...], a_ref[...], preferred_element_type=F32)
    aim_ref[...] += jnp.dot(s_ref[...], d_ref[...], preferred_element_type=F32)

    @pl.when(k == pl.num_programs(1) - 1)
    def _():
        grow = lax.broadcasted_iota(jnp.int32, are_ref.shape, 0) + pl.program_id(0) * are_ref.shape[0]
        kre_ref[...] = are_ref[...]
        kim_ref[...] = jnp.where(grow == 0, ny_ref[...], aim_ref[...])


def _filter_dft(ctab, stab, a, d, ny):
    seq, c = a.shape
    t = min(seq, 1024)
    return pl.pallas_call(
        _filter_dft_kernel,
        grid=(seq // t, seq // t),
        in_specs=[pl.BlockSpec((t, t), lambda m, k: (m, k)),
                  pl.BlockSpec((t, t), lambda m, k: (m, k)),
                  pl.BlockSpec((t, c), lambda m, k: (k, 0)),
                  pl.BlockSpec((t, c), lambda m, k: (k, 0)),
                  pl.BlockSpec((1, c), lambda m, k: (0, 0))],
        out_specs=[pl.BlockSpec((t, c), lambda m, k: (m, 0))] * 2,
        out_shape=[jax.ShapeDtypeStruct((seq, c), F32)] * 2,
        scratch_shapes=[pltpu.VMEM((t, c), F32), pltpu.VMEM((t, c), F32)],
        compiler_params=_params("parallel", "arbitrary"),
        name="hyena_filter_dft",
    )(ctab, stab, a, d, ny)


def _dft_fwd_kernel(c_ref, s_ref, v_ref, kre_ref, kim_ref, yre_ref, yim_ref, are_ref, aim_ref):
    k = pl.program_id(2)
    bt = v_ref.shape[0]

    @pl.when(k == 0)
    def _():
        are_ref[...] = jnp.zeros_like(are_ref)
        aim_ref[...] = jnp.zeros_like(aim_ref)

    for b in range(bt):
        vb = v_ref[b].astype(BF16)
        are_ref[b] += jnp.dot(c_ref[...], vb, preferred_element_type=F32)
        aim_ref[b] += jnp.dot(s_ref[...], vb, preferred_element_type=F32)

    @pl.when(k == pl.num_programs(2) - 1)
    def _():
        kre, kim = kre_ref[...], kim_ref[...]
        grow = lax.broadcasted_iota(jnp.int32, kre.shape, 0) + pl.program_id(1) * kre.shape[0]
        packed = grow == 0
        for b in range(bt):
            xr, xi = are_ref[b], aim_ref[b]
            yre = jnp.where(packed, 0.5 * xr * kre, xr * kre - xi * kim)
            yim = jnp.where(packed, 0.5 * xi * kim, xr * kim + xi * kre)
            yre_ref[b] = yre.astype(BF16)
            yim_ref[b] = yim.astype(BF16)


def _dft_fwd(ctab, stab, v, kre, kim, bt):
    bsz, seq, c = v.shape
    tm, tk = min(seq, DFT_M_TILE), min(seq, DFT_K_TILE)
    tab = pl.BlockSpec((tm, tk), lambda b, m, k: (m, k))
    kspec = pl.BlockSpec((tm, c), lambda b, m, k: (m, 0))
    out = pl.BlockSpec((bt, tm, c), lambda b, m, k: (b, m, 0))
    shp = jax.ShapeDtypeStruct((bsz, seq, c), BF16)
    return pl.pallas_call(
        _dft_fwd_kernel,
        grid=(bsz // bt, seq // tm, seq // tk),
        in_specs=[tab, tab, pl.BlockSpec((bt, tk, c), lambda b, m, k: (b, k, 0)), kspec, kspec],
        out_specs=[out, out],
        out_shape=[shp, shp],
        scratch_shapes=[pltpu.VMEM((bt, tm, c), F32), pltpu.VMEM((bt, tm, c), F32)],
        compiler_params=_params("parallel", "parallel", "arbitrary"),
        name="hyena_dft_fwd",
    )(ctab, stab, v, kre, kim)


def _dft_inv_kernel(c_ref, st_ref, yre_ref, yim_ref, v_ref, x0_ref, hb_ref, o_ref, acc_ref, *, scale):
    k = pl.program_id(2)
    bt = v_ref.shape[0]

    @pl.when(k == 0)
    def _():
        acc_ref[...] = jnp.zeros_like(acc_ref)

    for b in range(bt):
        acc_ref[b] += (jnp.dot(c_ref[...], yre_ref[b], preferred_element_type=F32)
                       + jnp.dot(st_ref[...], yim_ref[b], preferred_element_type=F32))

    @pl.when(k == pl.num_programs(2) - 1)
    def _():
        for b in range(bt):
            v = v_ref[b]
            y = (acc_ref[b] * scale + hb_ref[...] * v.astype(F32)) * x0_ref[b].astype(F32)
            o_ref[b] = y.astype(o_ref.dtype)


def _dft_inv(ctab, sttab, yre, yim, v, x0, hy_bias, bt):
    bsz, seq, c = v.shape
    tm, tk = min(seq, DFT_M_TILE), min(seq, DFT_K_TILE)
    tab = pl.BlockSpec((tm, tk), lambda b, m, k: (m, k))
    yspec = pl.BlockSpec((bt, tk, c), lambda b, m, k: (b, k, 0))
    tok = pl.BlockSpec((bt, tm, c), lambda b, m, k: (b, m, 0))
    return pl.pallas_call(
        functools.partial(_dft_inv_kernel, scale=1.0 / seq),
        grid=(bsz // bt, seq // tm, seq // tk),
        in_specs=[tab, tab, yspec, yspec, tok, tok, pl.BlockSpec((1, c), lambda b, m, k: (0, 0))],
        out_specs=tok,
        out_shape=jax.ShapeDtypeStruct((bsz, seq, c), BF16),
        scratch_shapes=[pltpu.VMEM((bt, tm, c), F32)],
        compiler_params=_params("parallel", "parallel", "arbitrary"),
        name="hyena_dft_inv",
    )(ctab, sttab, yre, yim, v, x0, hy_bias)


def _log_sigmoid(x):
    return jnp.minimum(x, 0.0) - jnp.log1p(jnp.exp(-jnp.abs(x)))


def _mlstm_kernel(*refs, heads, dh, zero_state):
    q_ref, k_ref, v_ref, g_ref, gt_ref = refs[:5]
    h_ref, c_out_ref, n_out_ref, m_out_ref, c_s, n_s, m_s = refs[-7:]
    step = pl.program_id(2)
    fwd = pl.program_id(0) == 0
    bt, t = q_ref.shape[0], q_ref.shape[1]

    @pl.when(step == 0)
    def _():
        if zero_state:
            c_s[...] = jnp.zeros_like(c_s)
            n_s[...] = jnp.zeros_like(n_s)
            m_s[...] = jnp.zeros_like(m_s)
        else:
            c0_ref, n0_ref, m0_ref = refs[5:8]
            c_s[...] = c0_ref[:, 0]
            n_s[...] = n0_ref[:, 0]
            m_s[...] = m0_ref[:, 0]

    row = lax.broadcasted_iota(jnp.int32, (t, t), 0)
    col = lax.broadcasted_iota(jnp.int32, (t, t), 1)
    sgn = jnp.where(fwd, 1, -1)
    seen = sgn * (row - col) >= 0
    seen_b = seen.astype(BF16)
    seen_tb = (sgn * (col - row) >= 0).astype(BF16)
    nt = (((1,), (1,)), ((), ()))
    tn = (((0,), (0,)), ((), ()))

    def split3(x, axis):
        hi = x.astype(BF16).astype(F32)
        r = x - hi
        mid = r.astype(BF16).astype(F32)
        return jnp.concatenate([hi, mid, r - mid], axis=axis).astype(BF16)

    g = g_ref[0, 0]
    gt = gt_ref[0, 0]
    lf = _log_sigmoid(g)
    lf_tot = jnp.sum(lf, axis=0, keepdims=True)
    cs = jnp.dot(seen_b, split3(lf, 1), preferred_element_type=F32)
    nl = lf.shape[1]
    bcol_all = cs[:, :nl] + cs[:, nl:2 * nl] + cs[:, 2 * nl:]
    rs = jnp.dot(split3(_log_sigmoid(gt), 0), seen_tb, preferred_element_type=F32)
    nr = gt.shape[0]
    brow_all = rs[:nr] + rs[nr:2 * nr] + rs[2 * nr:]
    chains = [(b, hd) for b in range(bt) for hd in range(heads)]
    sl = lambda hd: slice(hd * dh, (hd + 1) * dh)
    qs = [q_ref[b, :, sl(hd)] for b, hd in chains]
    ks = [k_ref[b, :, sl(hd)] for b, hd in chains]
    vs = [v_ref[b, :, sl(hd)].astype(BF16) for b, hd in chains]
    qk = [lax.dot_general(q, k, nt, preferred_element_type=F32) for q, k in zip(qs, ks)]
    each = lambda f, *ls: [f(*a) for a in zip(*ls)]
    fcols = [b * 2 * heads + heads + hd for b, hd in chains]
    bc = [bcol_all[:, c:c + 1] for c in fcols]
    rowt = [gt[c - heads:c - heads + 1, :] - brow_all[c:c + 1, :] for c in fcols]
    ic = [g[:, c - heads:c - heads + 1] for c in fcols]
    b_end = [lf_tot[:, c:c + 1] for c in fcols]
    m_prev = [m_s[b, hd:hd + 1, 0:1] for b, hd in chains]
    ct_prev = [c_s[b, hd] for b, hd in chains]
    n_prev = [n_s[b, hd:hd + 1, :] for b, hd in chains]
    dmat = each(lambda x, y: jnp.where(seen, x + y, -jnp.inf), bc, rowt)
    inter = each(lambda x, m: x + m, bc, m_prev)
    dmax = each(lambda d: jnp.max(d, axis=1, keepdims=True), dmat)
    m_t = each(jnp.maximum, inter, dmax)
    s = each(lambda r, d, m: r * jnp.exp(d - m), qk, dmat, m_t)
    w_inter = each(lambda i, m: jnp.exp(i - m), inter, m_t)
    qf = each(lambda q: q.astype(F32), qs)
    lhs = each(lambda x, q, w: jnp.concatenate([x.astype(BF16), (q * w).astype(BF16)], axis=1),
               s, qf, w_inter)
    rhs = each(lambda v, c: jnp.concatenate([v, c.astype(BF16)], axis=0), vs, ct_prev)
    nums = each(lambda l, r: jnp.dot(l, r, preferred_element_type=F32), lhs, rhs)
    qn = each(lambda q, n: jnp.sum(q * n, axis=1, keepdims=True), qf, n_prev)
    ssum = each(lambda x: jnp.sum(x, axis=1, keepdims=True), s)
    den = each(lambda a, w, x: a + w * x, ssum, w_inter, qn)
    w_log = each(lambda e, x, i: e - x + i, b_end, bc, ic)
    wmax = each(lambda w: jnp.max(w, axis=0, keepdims=True), w_log)
    m_new = each(lambda e, m, w: jnp.maximum(e + m, w), b_end, m_prev, wmax)
    w_s = each(lambda w, m: jnp.exp(w - m), w_log, m_new)
    w_c = each(lambda e, m, mn: jnp.exp(e + m - mn), b_end, m_prev, m_new)
    kw = each(lambda k, w: k.astype(F32) * w, ks, w_s)
    upds = each(lambda x, v: lax.dot_general(x.astype(BF16), v, tn, preferred_element_type=F32), kw, vs)
    ksum = each(lambda x: jnp.sum(x, axis=0, keepdims=True), kw)
    for i, (b, hd) in enumerate(chains):
        h_ref[0, b, :, sl(hd)] = nums[i] / jnp.maximum(jnp.abs(den[i]), jnp.exp(-m_t[i]))
        c_s[b, hd] = w_c[i] * ct_prev[i] + upds[i]
        n_s[b, hd:hd + 1, :] = w_c[i] * n_prev[i] + ksum[i]
        m_s[b, hd:hd + 1, :] = jnp.broadcast_to(m_new[i], (1, dh))

    @pl.when(step == pl.num_programs(2) - 1)
    def _():
        c_out_ref[:, 0] = c_s[...]
        n_out_ref[:, 0] = n_s[...]
        m_out_ref[:, 0] = m_s[...]


def _mlstm(qk, u_ml, g, c0, n0, m0, d_ml):
    bsz, seq, _ = qk.shape
    heads = ML_HEADS
    dh = d_ml // heads
    nc = seq // CHUNK
    bt = math.gcd(bsz, 4)
    nbb = bsz // bt
    gi = g[..., :4 * heads].reshape(nbb, bt, seq, 2, 2 * heads)
    g2 = jnp.transpose(gi, (3, 0, 2, 1, 4)).reshape(2, nbb, seq, bt * 2 * heads)
    gt2 = jnp.swapaxes(g2, 2, 3)
    g2 = jnp.pad(g2, ((0, 0), (0, 0), (0, 0), (0, LANE - bt * 2 * heads)))
    chunk = lambda d, c: c + d * (nc - 1 - 2 * c)
    tok = lambda j: pl.BlockSpec((bt, CHUNK, d_ml), lambda d, b, c: (b, chunk(d, c), j))
    st = lambda *tail: pl.BlockSpec((bt, 1, heads) + tail, lambda d, b, c: (b, d, 0) + (0,) * len(tail))
    zero_state = c0 is None
    states = [] if zero_state else [jnp.swapaxes(c0, -1, -2), n0,
                                    jnp.broadcast_to(m0[..., None], m0.shape + (dh,))]
    h, ct, n, m = pl.pallas_call(
        functools.partial(_mlstm_kernel, heads=heads, dh=dh, zero_state=zero_state),
        grid=(2, nbb, nc),
        in_specs=[tok(0), tok(1), tok(2),
                  pl.BlockSpec((1, 1, CHUNK, LANE), lambda d, b, c: (d, b, chunk(d, c), 0)),
                  pl.BlockSpec((1, 1, bt * 2 * heads, CHUNK), lambda d, b, c: (d, b, 0, chunk(d, c)))]
                 + ([] if zero_state else [st(dh, dh), st(dh), st(dh)]),
        out_specs=[pl.BlockSpec((1, bt, CHUNK, d_ml), lambda d, b, c: (d, b, chunk(d, c), 0)),
                   st(dh, dh), st(dh), st(dh)],
        out_shape=[jax.ShapeDtypeStruct((2, bsz, seq, d_ml), F32),
                   jax.ShapeDtypeStruct((bsz, 2, heads, dh, dh), F32),
                   jax.ShapeDtypeStruct((bsz, 2, heads, dh), F32),
                   jax.ShapeDtypeStruct((bsz, 2, heads, dh), F32)],
        scratch_shapes=[pltpu.VMEM((bt, heads, dh, dh), F32), pltpu.VMEM((bt, heads, dh), F32),
                        pltpu.VMEM((bt, heads, dh), F32)],
        compiler_params=_params("parallel", "parallel", "arbitrary"),
        name="mlstm_scan",
    )(qk, qk, u_ml, g2, gt2, *states)
    return h, jnp.swapaxes(ct, -1, -2), n, m[..., 0]


def _mix_out_kernel(yhy_ref, hf_ref, hb_ref, o_ref, x_ref, mod_ref, mlw_ref, wout_ref, fnw_ref,
                    x1_ref, h2_ref, *, heads, dh):
    m = mod_ref[0]
    h = hf_ref[0, 0] + hb_ref[0, 0]
    o = o_ref[0].astype(F32)
    mlw = mlw_ref[...]
    parts = [yhy_ref[0].astype(BF16)]
    for hd in range(heads):
        sl = slice(hd * dh, (hd + 1) * dh)
        parts.append((_rms(h[:, sl], mlw[:, sl]) * jax.nn.sigmoid(o[:, sl])).astype(BF16))
    y = jnp.concatenate(parts, axis=-1)
    x1 = x_ref[0] + m[2:3] * jnp.dot(y, wout_ref[...], preferred_element_type=F32)
    x1_ref[0] = x1
    h2_ref[0] = (_rms(x1, fnw_ref[...]) * (1.0 + m[4:5]) + m[3:4]).astype(BF16)


def _mix_out(y_hy, h2dir, u_ml, x, mod, mod_row, ml_norm_w, w_out, norm_ffn_w):
    bsz, seq, d = x.shape
    d_ml = h2dir.shape[-1]
    tb = min(seq, 512)
    tok = lambda n, j=0: pl.BlockSpec((1, tb, n), lambda b, i: (b, i, j))
    hdir = lambda dr: pl.BlockSpec((1, 1, tb, d_ml), lambda b, i: (dr, b, i, 0))
    full = lambda a: pl.BlockSpec(a.shape, lambda b, i: (0,) * a.ndim)
    return pl.pallas_call(
        functools.partial(_mix_out_kernel, heads=ML_HEADS, dh=d_ml // ML_HEADS),
        grid=(bsz, seq // tb),
        in_specs=[tok(y_hy.shape[-1]), hdir(0), hdir(1), tok(d_ml, 3), tok(d),
                  pl.BlockSpec((1, 6, d), lambda b, i: (mod_row(b), 0, 0)),
                  full(ml_norm_w), full(w_out), full(norm_ffn_w)],
        out_specs=[tok(d), tok(d)],
        out_shape=[jax.ShapeDtypeStruct((bsz, seq, d), F32),
                   jax.ShapeDtypeStruct((bsz, seq, d), BF16)],
        compiler_params=_params("parallel", "parallel"),
        name="mixer_out",
    )(y_hy, h2dir, h2dir, u_ml, x, mod, ml_norm_w, w_out, norm_ffn_w)


def _ffn_kernel(*refs, width, halo, final, nj):
    if halo:
        (top_ref, main_ref, bot_ref, x1_ref, mod_ref, wup_ref, cw_ref, cb_ref, wd_ref, fw_ref,
         o_ref, acc_ref) = refs
    else:
        (main_ref, x1_ref, mod_ref, wup_ref, cw_ref, cb_ref, wd_ref, fw_ref, o_ref, acc_ref) = refs
    d = main_ref.shape[-1]
    hm = main_ref[...].reshape(-1, d)
    m_rows = hm.shape[0]
    acc_ref[...] = jnp.zeros_like(acc_ref)

    col = lax.broadcasted_iota(jnp.int32, (m_rows, 1), 0) % width
    not_first = (col != 0).astype(F32)
    not_last = (col != width - 1).astype(F32)
    if halo:
        r = pl.program_id(1)
        top_ok = jnp.where(r > 0, 1.0, 0.0)
        bot_ok = jnp.where(r < pl.num_programs(1) - 1, 1.0, 0.0)
        ht, hb = top_ref[0], bot_ref[0]

    def conv(w, cw, cb):
        um = jnp.dot(hm, w, preferred_element_type=F32)
        slabs = [(1, um)]
        if halo:
            ut = jnp.dot(ht, w, preferred_element_type=F32) * top_ok
            ub = jnp.dot(hb, w, preferred_element_type=F32) * bot_ok
            full = jnp.concatenate([ut, um, ub], axis=0)
            slabs = [(kr, full[kr * width:kr * width + m_rows]) for kr in range(3)]
        if halo:
            slabs = [(kr, s.astype(BF16)) for kr, s in slabs]
            cw = cw.astype(BF16)
        left = sum(s * cw[3 * kr:3 * kr + 1] for kr, s in slabs).astype(F32)
        mid = sum(s * cw[3 * kr + 1:3 * kr + 2] for kr, s in slabs).astype(F32)
        right = sum(s * cw[3 * kr + 2:3 * kr + 3] for kr, s in slabs).astype(F32)
        return (mid + not_first * pltpu.roll(left, 1, 0)
                + not_last * pltpu.roll(right, m_rows - 1, 0) + cb)

    def tile(j, carry):
        a = conv(wup_ref[j], cw_ref[j], cb_ref[j])
        val = conv(wup_ref[nj + j], cw_ref[nj + j], cb_ref[nj + j])
        act = (_silu(a) * val).astype(BF16)
        acc_ref[...] += jnp.dot(act, wd_ref[j], preferred_element_type=F32)
        return carry

    lax.fori_loop(0, nj, tile, 0)
    m = mod_ref[0]
    x2 = x1_ref[...].reshape(-1, d) + m[5:6] * acc_ref[...]
    if final:
        x2 = _rms(x2, fw_ref[...])
    o_ref[...] = x2.reshape(o_ref.shape)


def _ffn(h2, x1, mod, mod_row, w_up, conv_w, conv_b, w_down, final_w, rows, final):
    bsz, seq, d = x1.shape
    nj = w_down.shape[0]
    width = seq // rows
    halo = rows > 1
    if halo:
        rb = 16
        nb, m_tok, nr = 1, rb * width, rows // rb
    else:
        nb, m_tok, nr = math.gcd(bsz, 4), seq, 1
    tok = pl.BlockSpec((nb, m_tok, d), lambda b, r: (b, r, 0))
    whole = lambda a: pl.BlockSpec(a.shape, lambda b, r: (0,) * a.ndim, pipeline_mode=pl.Buffered(1))
    in_specs, args = [], []
    if halo:
        in_specs += [pl.BlockSpec((1, width, d), lambda b, r: (b, jnp.maximum(r * rb - 1, 0), 0)),
                     tok,
                     pl.BlockSpec((1, width, d), lambda b, r: (b, jnp.minimum((r + 1) * rb, rows - 1), 0))]
        args += [h2, h2, h2]
    else:
        in_specs += [tok]
        args += [h2]
    in_specs += [tok, pl.BlockSpec((1, 6, d), lambda b, r: (mod_row(b), 0, 0)),
                 whole(w_up), whole(conv_w), whole(conv_b), whole(w_down), whole(final_w)]
    args += [x1, mod, w_up, conv_w, conv_b, w_down, final_w]
    return pl.pallas_call(
        functools.partial(_ffn_kernel, width=width, halo=halo, final=final, nj=nj),
        grid=(bsz // nb, nr),
        in_specs=in_specs,
        out_specs=tok,
        out_shape=jax.ShapeDtypeStruct((bsz, seq, d), F32),
        scratch_shapes=[pltpu.VMEM((nb * m_tok, d), F32)],
        compiler_params=_params("parallel", "parallel"),
        name="conv_ffn",
    )(*args)


def _layer(x, mod, mod_row, p, tabs, c0, n0, m0, rows, final):
    bsz, seq, d = x.shape
    d_hy, d_ml = p["d_hy"], p["d_ml"]
    heads = ML_HEADS
    ctab, stab, sttab = tabs
    u_hy, u_ml, g = _inproj(x, mod, mod_row, p["norm_mix_w"], p["w_hy"], p["w_ml"], p["w_g"], p["b_g"])
    v, x0 = _hy_pre(u_hy, p["hy_conv_w"], p["hy_conv_b"])
    a, dd, ny = _filter_taps(seq, p["hy_filt"], d_hy)
    kre, kim = _filter_dft(ctab, stab, a, dd, ny)
    bt = math.gcd(bsz, 4)
    yre, yim = _dft_fwd(ctab, stab, v, kre, kim, bt)
    y_hy = _dft_inv(ctab, sttab, yre, yim, v, x0, p["hy_bias"], bt)
    qk = _ml_pre(u_ml, p["ml_conv_w"], p["ml_conv_b"], d_ml, d_ml // heads)
    h2dir, c_new, n_new, m_new = _mlstm(qk, u_ml, g, c0, n0, m0, d_ml)
    x1, h2 = _mix_out(y_hy, h2dir, u_ml, x, mod, mod_row, p["ml_norm_w"], p["w_out"], p["norm_ffn_w"])
    x2 = _ffn(h2, x1, mod, mod_row, p["ffn_w_up"], p["ffn_conv_w"], p["ffn_conv_b"],
              p["ffn_w_down"], p["final_norm_w"], rows, final)
    return x2, c_new, n_new, m_new


def _layer_params(l, d_hy, d_ml, norm_mix_w, w_in, b_gate, hy_conv_w, hy_conv_b, hy_filt, hy_bias,
                  ml_conv_w, ml_conv_b, ml_norm_w, w_out, norm_ffn_w, ffn_w_up, ffn_conv_w,
                  ffn_conv_b, ffn_w_down, final_norm_w):
    w = w_in[l]
    n_gates = b_gate.shape[-1]
    d_ff = ffn_w_down.shape[1]
    wg = jnp.pad(w[:, 3 * d_hy + 4 * d_ml:], ((0, 0), (0, LANE - n_gates)))
    nj = d_ff // FFN_TILE
    tiles = lambda a: jnp.transpose(a.reshape(a.shape[0], 2 * nj, FFN_TILE), (1, 0, 2))
    return {
        "d_hy": d_hy, "d_ml": d_ml,
        "norm_mix_w": norm_mix_w[l][None, :],
        "w_hy": w[:, :3 * d_hy].astype(BF16),
        "w_ml": w[:, 3 * d_hy:3 * d_hy + 4 * d_ml].astype(BF16),
        "w_g": jnp.concatenate(_split_hi_lo(wg), axis=1),
        "b_g": jnp.pad(b_gate[l], (0, LANE - n_gates))[None, :],
        "hy_conv_w": hy_conv_w[l], "hy_conv_b": hy_conv_b[l][None, :],
        "hy_filt": tuple(a[l] for a in hy_filt),
        "hy_bias": hy_bias[l][None, :],
        "ml_conv_w": ml_conv_w[l], "ml_conv_b": ml_conv_b[l][None, :],
        "ml_norm_w": ml_norm_w[l][None, :],
        "w_out": w_out[l].astype(BF16),
        "norm_ffn_w": norm_ffn_w[l][None, :],
        "ffn_w_up": tiles(ffn_w_up[l].astype(BF16)),
        "ffn_conv_w": tiles(ffn_conv_w[l].reshape(9, 2 * d_ff)),
        "ffn_conv_b": ffn_conv_b[l].reshape(2 * nj, 1, FFN_TILE),
        "ffn_w_down": ffn_w_down[l].astype(BF16).reshape(nj, FFN_TILE, -1),
        "final_norm_w": final_norm_w[None, :],
    }


def kernel(x_prompt, x_sample, state_mlstm_C, state_mlstm_n, state_mlstm_m, c, c_ctx, ada_w, ada_b, norm_mix_w, w_in, b_gate, hy_conv_w, hy_conv_b, hy_filt_w1, hy_filt_b1, hy_filt_w2, hy_filt_b2, hy_filt_w3, hy_filt_b3, hy_filt_w4, hy_freq, hy_bias, ml_conv_w, ml_conv_b, ml_norm_w, w_out, norm_ffn_w, ffn_w_up, ffn_conv_w, ffn_conv_b, ffn_w_down, final_norm_w):
    depth, d, _ = ada_w.shape
    bp, seq_p, _ = x_prompt.shape
    bs, seq_s, _ = x_sample.shape
    heads = ML_HEADS
    d_hy = hy_bias.shape[-1]
    d_ml = ml_norm_w.shape[-1]
    dh = d_ml // heads
    assert bs < MOD_ROWS and seq_s % GRID_W == 0

    cvec = jnp.concatenate([c, c_ctx[None, :], jnp.zeros((MOD_ROWS - bs - 1, d), F32)], axis=0)
    mod_all = _ada(cvec, ada_w, ada_b).reshape(depth, MOD_ROWS, 6, d)
    row_ctx = lambda b: bs
    row_lat = lambda b: b

    tabs_p = _dft_tables(seq_p)
    tabs_s = _dft_tables(seq_s)
    hy_filt = (hy_filt_w1, hy_filt_b1, hy_filt_w2, hy_filt_b2, hy_filt_w3, hy_filt_b3, hy_filt_w4, hy_freq)

    xp, xs = x_prompt, x_sample
    new_c, new_n, new_m = [], [], []
    for l in range(depth):
        p = _layer_params(l, d_hy, d_ml, norm_mix_w, w_in, b_gate, hy_conv_w, hy_conv_b, hy_filt,
                          hy_bias, ml_conv_w, ml_conv_b, ml_norm_w, w_out, norm_ffn_w, ffn_w_up,
                          ffn_conv_w, ffn_conv_b, ffn_w_down, final_norm_w)
        final = l == depth - 1
        xp, cl, nl, ml = _layer(xp, mod_all[l], row_ctx, p, tabs_p, None, None, None, 1, final)
        new_c.append(cl)
        new_n.append(nl)
        new_m.append(ml)
        xs, _, _, _ = _layer(xs, mod_all[l], row_lat, p, tabs_s, state_mlstm_C[:, l],
                             state_mlstm_n[:, l], state_mlstm_m[:, l], seq_s // GRID_W, final)
    return (xp, xs, jnp.stack(new_c, axis=1), jnp.stack(new_n, axis=1), jnp.stack(new_m, axis=1))
```

```python
import functools
import math

import numpy as np
import jax
import jax.numpy as jnp
from jax import lax
from jax.experimental import pallas as pl
from jax.experimental.pallas import tpu as pltpu

F32 = jnp.float32
BF16 = jnp.bfloat16
HIGHEST = lax.Precision.HIGHEST

GRID_W = 64
ML_HEADS = 4
CHUNK = 128
HY_BANDS = 16
HY_DECAY_TARGET = 1e-2
HY_FAST_DECAY_PCT = 0.3
HY_SLOW_DECAY_PCT = 1.5
HY_MIN_DECAY = math.log(HY_DECAY_TARGET) / HY_SLOW_DECAY_PCT
HY_MAX_DECAY = math.log(HY_DECAY_TARGET) / HY_FAST_DECAY_PCT
EPS = 1e-6

LANE = 128
HALO_ROWS = 16
VMEM_LIMIT = 56 * 1024 * 1024
MOD_ROWS = 8
DFT_ROW_TILE = 64
FFN_TILE = 256
DFT_M_TILE = 512


def _params(*sem):
    return pltpu.CompilerParams(dimension_semantics=sem, vmem_limit_bytes=VMEM_LIMIT)


def _silu(x):
    return x * jax.nn.sigmoid(x)


def _rms(x, w):
    return x * lax.rsqrt(jnp.mean(x * x, axis=-1, keepdims=True) + EPS) * w


def _split_hi_lo(x):
    hi = x.astype(BF16)
    return hi, (x - hi.astype(F32)).astype(BF16)


def _ada_kernel(c_ref, w_ref, b_ref, o_ref):
    s = _silu(c_ref[...])
    o_ref[0] = jnp.dot(s, w_ref[0], precision=HIGHEST, preferred_element_type=F32) + b_ref[0]


def _ada(cvec, ada_w, ada_b):
    depth, d, nmod = ada_w.shape
    tn = 1024
    return pl.pallas_call(
        _ada_kernel,
        grid=(depth, nmod // tn),
        in_specs=[
            pl.BlockSpec((MOD_ROWS, d), lambda l, j: (0, 0)),
            pl.BlockSpec((1, d, tn), lambda l, j: (l, 0, j)),
            pl.BlockSpec((1, 1, tn), lambda l, j: (l, 0, j)),
        ],
        out_specs=pl.BlockSpec((1, MOD_ROWS, tn), lambda l, j: (l, 0, j)),
        out_shape=jax.ShapeDtypeStruct((depth, MOD_ROWS, nmod), F32),
        compiler_params=_params("parallel", "parallel"),
        name="ada_mod",
    )(cvec, ada_w, ada_b.reshape(depth, 1, nmod))


def _inproj_kernel(x_ref, mod_ref, nw_ref, why_ref, wml_ref, wg_ref, bg_ref,
                   uhy_ref, uml_ref, g_ref):
    m = mod_ref[0]
    h = _rms(x_ref[0], nw_ref[...]) * (1.0 + m[1:2]) + m[0:1]
    h_hi, h_lo = _split_hi_lo(h)
    uhy_ref[0] = jnp.dot(h_hi, why_ref[...], preferred_element_type=F32).astype(uhy_ref.dtype)
    uml_ref[0] = jnp.dot(h_hi, wml_ref[...], preferred_element_type=F32).astype(uml_ref.dtype)
    wg = wg_ref[...]
    ng = g_ref.shape[-1]
    a = jnp.dot(h_hi, wg, preferred_element_type=F32)
    b = jnp.dot(h_lo, wg[:, :ng], preferred_element_type=F32)
    g_ref[0] = a[:, :ng] + a[:, ng:] + b + bg_ref[...]


def _inproj(x, mod, mod_row, nw, w_hy, w_ml, w_g, b_g):
    bsz, seq, d = x.shape
    tb = min(seq, 512)
    n_hy, n_ml = w_hy.shape[1], w_ml.shape[1]
    tok = lambda n: pl.BlockSpec((1, tb, n), lambda b, i: (b, i, 0))
    full = lambda a: pl.BlockSpec(a.shape, lambda b, i: (0,) * a.ndim)
    return pl.pallas_call(
        _inproj_kernel,
        grid=(bsz, seq // tb),
        in_specs=[tok(d), pl.BlockSpec((1, 6, d), lambda b, i: (mod_row(b), 0, 0)),
                  full(nw), full(w_hy), full(w_ml), full(w_g), full(b_g)],
        out_specs=[tok(n_hy), tok(n_ml), tok(LANE)],
        out_shape=[jax.ShapeDtypeStruct((bsz, seq, n_hy), BF16),
                   jax.ShapeDtypeStruct((bsz, seq, n_ml), BF16),
                   jax.ShapeDtypeStruct((bsz, seq, LANE), F32)],
        compiler_params=_params("parallel", "parallel"),
        name="inproj",
    )(x, mod, nw, w_hy, w_ml, w_g, b_g)


def _dwconv3(u, before, after, w, b):
    n = u.shape[0]
    row = lax.broadcasted_iota(jnp.int32, u.shape, 0)
    prev = jnp.where(row == 0, before, pltpu.roll(u, 1, 0))
    nxt = jnp.where(row == n - 1, after, pltpu.roll(u, n - 1, 0))
    return prev * w[0:1] + u * w[1:2] + nxt * w[2:3] + b


def _tile_neighbours(prev_ref, next_ref):
    i = pl.program_id(1)
    before = jnp.where(i == 0, 0.0, prev_ref[0].astype(F32)[HALO_ROWS - 1:HALO_ROWS])
    after = jnp.where(i == pl.num_programs(1) - 1, 0.0, next_ref[0].astype(F32)[0:1])
    return before, after


def _row_tile_specs(seq, tb, c, cblk=0):
    per = tb // HALO_ROWS
    last = seq // HALO_ROWS - 1
    main = pl.BlockSpec((1, tb, c), lambda b, i: (b, i, cblk))
    prev = pl.BlockSpec((1, HALO_ROWS, c), lambda b, i: (b, jnp.maximum(i * per - 1, 0), cblk))
    nxt = pl.BlockSpec((1, HALO_ROWS, c), lambda b, i: (b, jnp.minimum((i + 1) * per, last), cblk))
    return main, prev, nxt


def _hy_pre_kernel(u_ref, up_ref, un_ref, w_ref, b_ref, v_ref, x0_ref):
    before, after = _tile_neighbours(up_ref, un_ref)
    y = _dwconv3(u_ref[0].astype(F32), before, after, w_ref[...], b_ref[...])
    c = v_ref.shape[-1]
    x0_ref[0] = y[:, :c].astype(x0_ref.dtype)
    v_ref[0] = (y[:, 2 * c:] * y[:, c:2 * c]).astype(v_ref.dtype)


def _hy_pre(u_hy, conv_w, conv_b):
    bsz, seq, c3 = u_hy.shape
    tb = min(seq, 512)
    main, prev, nxt = _row_tile_specs(seq, tb, c3)
    full = lambda a: pl.BlockSpec(a.shape, lambda b, i: (0,) * a.ndim)
    out = pl.BlockSpec((1, tb, c3 // 3), lambda b, i: (b, i, 0))
    shp = jax.ShapeDtypeStruct((bsz, seq, c3 // 3), BF16)
    return pl.pallas_call(
        _hy_pre_kernel,
        grid=(bsz, seq // tb),
        in_specs=[main, prev, nxt, full(conv_w), full(conv_b)],
        out_specs=[out, out],
        out_shape=[shp, shp],
        compiler_params=_params("parallel", "parallel"),
        name="hyena_pre",
    )(u_hy, u_hy, u_hy, conv_w, conv_b)


def _ml_pre_kernel(u_ref, up_ref, un_ref, w_ref, b_ref, o_ref, *, d_ml, kscale):
    before, after = _tile_neighbours(up_ref, un_ref)
    y = _silu(_dwconv3(u_ref[0].astype(F32), before, after, w_ref[...], b_ref[...]))
    lane = lax.broadcasted_iota(jnp.int32, (1, y.shape[1]), 1)
    o_ref[0] = (y * jnp.where(lane >= d_ml, kscale, 1.0)).astype(o_ref.dtype)


def _ml_pre(u_ml, conv_w, conv_b, d_ml, dh):
    bsz, seq, _ = u_ml.shape
    tb = min(seq, 512)
    main, prev, nxt = _row_tile_specs(seq, tb, 2 * d_ml)
    full = lambda a: pl.BlockSpec(a.shape, lambda b, i: (0,) * a.ndim)
    return pl.pallas_call(
        functools.partial(_ml_pre_kernel, d_ml=d_ml, kscale=dh ** -0.5),
        grid=(bsz, seq // tb),
        in_specs=[main, prev, nxt, full(conv_w), full(conv_b)],
        out_specs=pl.BlockSpec((1, tb, 2 * d_ml), lambda b, i: (b, i, 0)),
        out_shape=jax.ShapeDtypeStruct((bsz, seq, 2 * d_ml), BF16),
        compiler_params=_params("parallel", "parallel"),
        name="mlstm_pre",
    )(u_ml, u_ml, u_ml, conv_w, conv_b)


def _dft_table_kernel(pre_ref, pim_ref, qre_ref, qim_ref, c_ref, s_ref, st_ref):
    pre, pim = pre_ref[0], pim_ref[0]
    qre, qim = qre_ref[...], qim_ref[...]
    c = pre * qre - pim * qim
    s = pre * qim + pim * qre
    grow = lax.broadcasted_iota(jnp.int32, c.shape, 0) + pl.program_id(0) * c.shape[0]
    lane = lax.broadcasted_iota(jnp.int32, c.shape, 1)
    alt_lane = (1 - 2 * (lane & 1)).astype(F32)
    alt_row = (1 - 2 * (grow & 1)).astype(F32)
    c_ref[...] = c.astype(BF16)
    s_ref[...] = jnp.where(grow == 0, alt_lane, s).astype(BF16)
    st_ref[...] = jnp.where(lane == 0, alt_row, s).astype(BF16)


def _dft_tables(seq):
    n = 2 * seq
    tm = DFT_ROW_TILE
    s = jnp.arange(seq, dtype=jnp.int32)[None, :]
    ang = lambda f: (-2.0 * math.pi / n) * ((f * s) % n).astype(F32)
    fp = (jnp.arange(seq // tm, dtype=jnp.int32) * tm)[:, None]
    fq = jnp.arange(tm, dtype=jnp.int32)[:, None]
    ap, aq = ang(fp), ang(fq)
    p3 = lambda a: a.reshape(seq // tm, 1, seq)
    pspec = pl.BlockSpec((1, 1, seq), lambda r: (r, 0, 0))
    qspec = pl.BlockSpec((tm, seq), lambda r: (0, 0))
    out = pl.BlockSpec((tm, seq), lambda r: (r, 0))
    shp = jax.ShapeDtypeStruct((seq, seq), BF16)
    return pl.pallas_call(
        _dft_table_kernel,
        grid=(seq // tm,),
        in_specs=[pspec, pspec, qspec, qspec],
        out_specs=[out, out, out],
        out_shape=[shp, shp, shp],
        compiler_params=_params("parallel"),
        name="dft_tables",
    )(p3(jnp.cos(ap)), p3(jnp.sin(ap)), jnp.cos(aq), jnp.sin(aq))


def _filter_kernel(z_ref, w1_ref, b1_ref, w2_ref, b2_ref, w3_ref, b3_ref, fr_ref,
                   w4_ref, dl_ref, a_ref, d_ref, ny_ref, h_ref):
    def dot3(x, w):
        x_hi, x_lo = _split_hi_lo(x)
        w_hi, w_lo = _split_hi_lo(w)
        mm = functools.partial(jnp.dot, preferred_element_type=F32)
        return mm(x_hi, w_hi) + mm(x_hi, w_lo) + mm(x_lo, w_hi)

    @pl.when(pl.program_id(0) == 0)
    def _():
        fr = fr_ref[...]
        h = jnp.sin(fr * (dot3(z_ref[...], w1_ref[...]) + b1_ref[...]))
        h = jnp.sin(fr * (dot3(h, w2_ref[...]) + b2_ref[...]))
        h_ref[...] = jnp.sin(fr * (dot3(h, w3_ref[...]) + b3_ref[...]))

    c = a_ref.shape[1]
    hfb = dot3(h_ref[...], w4_ref[0])
    decay = jnp.exp(-z_ref[:, 0:1] * dl_ref[...])
    hf = hfb[:, :c] * decay
    hb = hfb[:, c:] * decay
    row = lax.broadcasted_iota(jnp.int32, hf.shape, 0)
    hb = jnp.where(row == 0, 0.0, hb)
    inv = 1.0 / jnp.sum(jnp.abs(hf) + jnp.abs(hb), axis=0, keepdims=True)
    a = (hf + hb) * inv
    a_ref[...] = a.astype(BF16)
    d_ref[...] = ((hf - hb) * inv).astype(BF16)
    ny_ref[...] = jnp.sum(a * (1 - 2 * (row & 1)).astype(F32), axis=0, keepdims=True)


def _filter_taps(seq, filt, d_hy):
    w1, b1, w2, b2, w3, b3, w4, freq = filt
    emb, ffn = w1.shape
    t = jnp.linspace(0.0, 1.0, seq, dtype=F32)[:, None]
    wpos = (2.0 * math.pi / seq) * jnp.arange(seq, dtype=F32)[:, None]
    bands = jnp.linspace(1e-4, HY_BANDS - 1, HY_BANDS, dtype=F32)[None, :]
    z = jnp.concatenate([t, jnp.cos(bands * wpos), -jnp.sin(bands * wpos),
                         jnp.zeros((seq, LANE - emb), F32)], axis=-1)
    w1p = jnp.concatenate([w1, jnp.zeros((LANE - emb, ffn), F32)], axis=0)
    deltas = jnp.abs(jnp.linspace(HY_MIN_DECAY, HY_MAX_DECAY, d_hy, dtype=F32))[None, :]
    row = lambda a: a.reshape(1, -1)
    nblk = d_hy // LANE
    w4t = jnp.concatenate([w4[:, :d_hy].reshape(ffn, nblk, LANE), w4[:, d_hy:].reshape(ffn, nblk, LANE)], axis=2)
    w4t = jnp.transpose(w4t, (1, 0, 2))
    full = lambda a: pl.BlockSpec(a.shape, lambda j: (0,) * a.ndim)
    args = [z, w1p, row(b1), w2, row(b2), w3, row(b3), row(freq)]
    col = pl.BlockSpec((seq, LANE), lambda j: (0, j))
    vec = pl.BlockSpec((1, LANE), lambda j: (0, j))
    return pl.pallas_call(
        _filter_kernel,
        grid=(nblk,),
        in_specs=[full(a) for a in args] + [pl.BlockSpec((1, ffn, 2 * LANE), lambda j: (j, 0, 0)), vec],
        out_specs=[col, col, vec],
        out_shape=[jax.ShapeDtypeStruct((seq, d_hy), BF16),
                   jax.ShapeDtypeStruct((seq, d_hy), BF16),
                   jax.ShapeDtypeStruct((1, d_hy), F32)],
        scratch_shapes=[pltpu.VMEM((seq, ffn), F32)],
        compiler_params=_params("arbitrary"),
        name="hyena_filter",
    )(*args, w4t, deltas)


def _filter_dft_kernel(c_ref, s_ref, a_ref, d_ref, ny_ref, kre_ref, kim_ref, are_ref, aim_ref):
    k = pl.program_id(1)

    @pl.when(k == 0)
    def _():
        are_ref[...] = jnp.zeros_like(are_ref)
        aim_ref[...] = jnp.zeros_like(aim_ref)

    are_ref[...] += jnp.dot(c_ref[...], a_ref[...], preferred_element_type=F32)
    aim_ref[...] += jnp.dot(s_ref[...], d_ref[...], preferred_element_type=F32)

    @pl.when(k == pl.num_programs(1) - 1)
    def _():
        grow = lax.broadcasted_iota(jnp.int32, are_ref.shape, 0) + pl.program_id(0) * are_ref.shape[0]
        kre_ref[...] = are_ref[...]
        kim_ref[...] = jnp.where(grow == 0, ny_ref[...], aim_ref[...])


def _filter_dft(ctab, stab, a, d, ny):
    seq, c = a.shape
    t = min(seq, 1024)
    return pl.pallas_call(
        _filter_dft_kernel,
        grid=(seq // t, seq // t),
        in_specs=[pl.BlockSpec((t, t), lambda m, k: (m, k)),
                  pl.BlockSpec((t, t), lambda m, k: (m, k)),
                  pl.BlockSpec((t, c), lambda m, k: (k, 0)),
                  pl.BlockSpec((t, c), lambda m, k: (k, 0)),
                  pl.BlockSpec((1, c), lambda m, k: (0, 0))],
        out_specs=[pl.BlockSpec((t, c), lambda m, k: (m, 0))] * 2,
        out_shape=[jax.ShapeDtypeStruct((seq, c), F32)] * 2,
        scratch_shapes=[pltpu.VMEM((t, c), F32), pltpu.VMEM((t, c), F32)],
        compiler_params=_params("parallel", "arbitrary"),
        name="hyena_filter_dft",
    )(ctab, stab, a, d, ny)


def _dft_fwd_kernel(c_ref, s_ref, v_ref, kre_ref, kim_ref, yre_ref, yim_ref):
    kre, kim = kre_ref[...], kim_ref[...]
    grow = lax.broadcasted_iota(jnp.int32, kre.shape, 0) + pl.program_id(1) * kre.shape[0]
    packed = grow == 0
    for b in range(v_ref.shape[0]):
        vb = v_ref[b]
        xr = jnp.dot(c_ref[...], vb, preferred_element_type=F32)
        xi = jnp.dot(s_ref[...], vb, preferred_element_type=F32)
        yre = jnp.where(packed, 0.5 * xr * kre, xr * kre - xi * kim)
        yim = jnp.where(packed, 0.5 * xi * kim, xr * kim + xi * kre)
        yre_ref[b] = yre.astype(BF16)
        yim_ref[b] = yim.astype(BF16)


def _dft_fwd(ctab, stab, v, kre, kim, bt):
    bsz, seq, c = v.shape
    tm = min(seq, DFT_M_TILE)
    tab = pl.BlockSpec((tm, seq), lambda b, m: (m, 0))
    kspec = pl.BlockSpec((tm, c), lambda b, m: (m, 0))
    out = pl.BlockSpec((bt, tm, c), lambda b, m: (b, m, 0))
    shp = jax.ShapeDtypeStruct((bsz, seq, c), BF16)
    return pl.pallas_call(
        _dft_fwd_kernel,
        grid=(bsz // bt, seq // tm),
        in_specs=[tab, tab, pl.BlockSpec((bt, seq, c), lambda b, m: (b, 0, 0)), kspec, kspec],
        out_specs=[out, out],
        out_shape=[shp, shp],
        compiler_params=_params("parallel", "parallel"),
        name="hyena_dft_fwd",
    )(ctab, stab, v, kre, kim)


def _dft_inv_kernel(c_ref, st_ref, yre_ref, yim_ref, v_ref, x0_ref, hb_ref, o_ref, *, scale):
    for b in range(v_ref.shape[0]):
        conv = (jnp.dot(c_ref[...], yre_ref[b], preferred_element_type=F32)
                + jnp.dot(st_ref[...], yim_ref[b], preferred_element_type=F32))
        y = (conv * scale + hb_ref[...] * v_ref[b].astype(F32)) * x0_ref[b].astype(F32)
        o_ref[b] = y.astype(o_ref.dtype)


def _dft_inv(ctab, sttab, yre, yim, v, x0, hy_bias, bt):
    bsz, seq, c = v.shape
    tm = min(seq, DFT_M_TILE)
    tab = pl.BlockSpec((tm, seq), lambda b, m: (m, 0))
    yspec = pl.BlockSpec((bt, seq, c), lambda b, m: (b, 0, 0))
    tok = pl.BlockSpec((bt, tm, c), lambda b, m: (b, m, 0))
    return pl.pallas_call(
        functools.partial(_dft_inv_kernel, scale=1.0 / seq),
        grid=(bsz // bt, seq // tm),
        in_specs=[tab, tab, yspec, yspec, tok, tok, pl.BlockSpec((1, c), lambda b, m: (0, 0))],
        out_specs=tok,
        out_shape=jax.ShapeDtypeStruct((bsz, seq, c), BF16),
        compiler_params=_params("parallel", "parallel"),
        name="hyena_dft_inv",
    )(ctab, sttab, yre, yim, v, x0, hy_bias)


def _log_sigmoid(x):
    return jnp.minimum(x, 0.0) - jnp.log1p(jnp.exp(-jnp.abs(x)))


def _mlstm_kernel(*refs, heads, dh, zero_state):
    q_ref, k_ref, v_ref, g_ref, gt_ref = refs[:5]
    h_ref, c_out_ref, n_out_ref, m_out_ref, c_s, n_s, m_s = refs[-7:]
    step = pl.program_id(2)
    fwd = pl.program_id(0) == 0
    bt, t = q_ref.shape[0], q_ref.shape[1]

    @pl.when(step == 0)
    def _():
        if zero_state:
            c_s[...] = jnp.zeros_like(c_s)
            n_s[...] = jnp.zeros_like(n_s)
            m_s[...] = jnp.zeros_like(m_s)
        else:
            c0_ref, n0_ref, m0_ref = refs[5:8]
            c_s[...] = c0_ref[:, 0]
            n_s[...] = n0_ref[:, 0]
            m_s[...] = m0_ref[:, 0]

    row = lax.broadcasted_iota(jnp.int32, (t, t), 0)
    col = lax.broadcasted_iota(jnp.int32, (t, t), 1)
    sgn = jnp.where(fwd, 1, -1)
    seen = sgn * (row - col) >= 0
    seen_b = seen.astype(BF16)
    seen_tb = (sgn * (col - row) >= 0).astype(BF16)
    nt = (((1,), (1,)), ((), ()))
    tn = (((0,), (0,)), ((), ()))

    def split3(x, axis):
        hi = x.astype(BF16).astype(F32)
        r = x - hi
        mid = r.astype(BF16).astype(F32)
        return jnp.concatenate([hi, mid, r - mid], axis=axis).astype(BF16)

    g = g_ref[0, 0]
    gt = gt_ref[0, 0]
    lf = _log_sigmoid(g)
    lf_tot = jnp.sum(lf, axis=0, keepdims=True)
    cs = jnp.dot(seen_b, split3(lf, 1), preferred_element_type=F32)
    nl = lf.shape[1]
    bcol_all = cs[:, :nl] + cs[:, nl:2 * nl] + cs[:, 2 * nl:]
    rs = jnp.dot(split3(_log_sigmoid(gt), 0), seen_tb, preferred_element_type=F32)
    nr = gt.shape[0]
    brow_all = rs[:nr] + rs[nr:2 * nr] + rs[2 * nr:]
    chains = [(b, hd) for b in range(bt) for hd in range(heads)]
    sl = lambda hd: slice(hd * dh, (hd + 1) * dh)
    qs = [q_ref[b, :, sl(hd)] for b, hd in chains]
    ks = [k_ref[b, :, sl(hd)] for b, hd in chains]
    vs = [v_ref[b, :, sl(hd)].astype(BF16) for b, hd in chains]
    qk = [lax.dot_general(q, k, nt, preferred_element_type=F32) for q, k in zip(qs, ks)]
    each = lambda f, *ls: [f(*a) for a in zip(*ls)]
    fcols = [b * 2 * heads + heads + hd for b, hd in chains]
    bc = [bcol_all[:, c:c + 1] for c in fcols]
    rowt = [gt[c - heads:c - heads + 1, :] - brow_all[c:c + 1, :] for c in fcols]
    ic = [g[:, c - heads:c - heads + 1] for c in fcols]
    b_end = [lf_tot[:, c:c + 1] for c in fcols]
    m_prev = [m_s[b, hd:hd + 1, 0:1] for b, hd in chains]
    ct_prev = [c_s[b, hd] for b, hd in chains]
    n_prev = [n_s[b, hd:hd + 1, :] for b, hd in chains]
    dmat = each(lambda x, y: jnp.where(seen, x + y, -jnp.inf), bc, rowt)
    inter = each(lambda x, m: x + m, bc, m_prev)
    dmax = each(lambda d: jnp.max(d, axis=1, keepdims=True), dmat)
    m_t = each(jnp.maximum, inter, dmax)
    s = each(lambda r, d, m: r * jnp.exp(d - m), qk, dmat, m_t)
    w_inter = each(lambda i, m: jnp.exp(i - m), inter, m_t)
    qf = each(lambda q: q.astype(F32), qs)
    lhs = each(lambda x, q, w: jnp.concatenate([x.astype(BF16), (q * w).astype(BF16)], axis=1),
               s, qf, w_inter)
    rhs = each(lambda v, c: jnp.concatenate([v, c.astype(BF16)], axis=0), vs, ct_prev)
    nums = each(lambda l, r: jnp.dot(l, r, preferred_element_type=F32), lhs, rhs)
    qn = each(lambda q, n: jnp.sum(q * n, axis=1, keepdims=True), qf, n_prev)
    ssum = each(lambda x: jnp.sum(x, axis=1, keepdims=True), s)
    den = each(lambda a, w, x: a + w * x, ssum, w_inter, qn)
    w_log = each(lambda e, x, i: e - x + i, b_end, bc, ic)
    wmax = each(lambda w: jnp.max(w, axis=0, keepdims=True), w_log)
    m_new = each(lambda e, m, w: jnp.maximum(e + m, w), b_end, m_prev, wmax)
    w_s = each(lambda w, m: jnp.exp(w - m), w_log, m_new)
    w_c = each(lambda e, m, mn: jnp.exp(e + m - mn), b_end, m_prev, m_new)
    kw = each(lambda k, w: k.astype(F32) * w, ks, w_s)
    upds = each(lambda x, v: lax.dot_general(x.astype(BF16), v, tn, preferred_element_type=F32), kw, vs)
    ksum = each(lambda x: jnp.sum(x, axis=0, keepdims=True), kw)
    for i, (b, hd) in enumerate(chains):
        h_ref[0, b, :, sl(hd)] = nums[i] / jnp.maximum(jnp.abs(den[i]), jnp.exp(-m_t[i]))
        c_s[b, hd] = w_c[i] * ct_prev[i] + upds[i]
        n_s[b, hd:hd + 1, :] = w_c[i] * n_prev[i] + ksum[i]
        m_s[b, hd:hd + 1, :] = jnp.broadcast_to(m_new[i], (1, dh))

    @pl.when(step == pl.num_programs(2) - 1)
    def _():
        c_out_ref[:, 0] = c_s[...]
        n_out_ref[:, 0] = n_s[...]
        m_out_ref[:, 0] = m_s[...]


def _mlstm(qk, u_ml, g, c0, n0, m0, d_ml):
    bsz, seq, _ = qk.shape
    heads = ML_HEADS
    dh = d_ml // heads
    nc = seq // CHUNK
    bt = math.gcd(bsz, 4)
    nbb = bsz // bt
    gi = g[..., :4 * heads].reshape(nbb, bt, seq, 2, 2 * heads)
    g2 = jnp.transpose(gi, (3, 0, 2, 1, 4)).reshape(2, nbb, seq, bt * 2 * heads)
    gt2 = jnp.swapaxes(g2, 2, 3)
    g2 = jnp.pad(g2, ((0, 0), (0, 0), (0, 0), (0, LANE - bt * 2 * heads)))
    chunk = lambda d, c: c + d * (nc - 1 - 2 * c)
    tok = lambda j: pl.BlockSpec((bt, CHUNK, d_ml), lambda d, b, c: (b, chunk(d, c), j))
    st = lambda *tail: pl.BlockSpec((bt, 1, heads) + tail, lambda d, b, c: (b, d, 0) + (0,) * len(tail))
    zero_state = c0 is None
    states = [] if zero_state else [jnp.swapaxes(c0, -1, -2), n0,
                                    jnp.broadcast_to(m0[..., None], m0.shape + (dh,))]
    h, ct, n, m = pl.pallas_call(
        functools.partial(_mlstm_kernel, heads=heads, dh=dh, zero_state=zero_state),
        grid=(2, nbb, nc),
        in_specs=[tok(0), tok(1), tok(2),
                  pl.BlockSpec((1, 1, CHUNK, LANE), lambda d, b, c: (d, b, chunk(d, c), 0)),
                  pl.BlockSpec((1, 1, bt * 2 * heads, CHUNK), lambda d, b, c: (d, b, 0, chunk(d, c)))]
                 + ([] if zero_state else [st(dh, dh), st(dh), st(dh)]),
        out_specs=[pl.BlockSpec((1, bt, CHUNK, d_ml), lambda d, b, c: (d, b, chunk(d, c), 0)),
                   st(dh, dh), st(dh), st(dh)],
        out_shape=[jax.ShapeDtypeStruct((2, bsz, seq, d_ml), F32),
                   jax.ShapeDtypeStruct((bsz, 2, heads, dh, dh), F32),
                   jax.ShapeDtypeStruct((bsz, 2, heads, dh), F32),
                   jax.ShapeDtypeStruct((bsz, 2, heads, dh), F32)],
        scratch_shapes=[pltpu.VMEM((bt, heads, dh, dh), F32), pltpu.VMEM((bt, heads, dh), F32),
                        pltpu.VMEM((bt, heads, dh), F32)],
        compiler_params=_params("parallel", "parallel", "arbitrary"),
        name="mlstm_scan",
    )(qk, qk, u_ml, g2, gt2, *states)
    return h, jnp.swapaxes(ct, -1, -2), n, m[..., 0]


def _mix_out_kernel(yhy_ref, hf_ref, hb_ref, o_ref, x_ref, mod_ref, mlw_ref, wout_ref, fnw_ref,
                    x1_ref, h2_ref, *, heads, dh):
    m = mod_ref[0]
    h = hf_ref[0, 0] + hb_ref[0, 0]
    o = o_ref[0].astype(F32)
    mlw = mlw_ref[...]
    parts = [yhy_ref[0].astype(BF16)]
    for hd in range(heads):
        sl = slice(hd * dh, (hd + 1) * dh)
        parts.append((_rms(h[:, sl], mlw[:, sl]) * jax.nn.sigmoid(o[:, sl])).astype(BF16))
    y = jnp.concatenate(parts, axis=-1)
    x1 = x_ref[0] + m[2:3] * jnp.dot(y, wout_ref[...], preferred_element_type=F32)
    x1_ref[0] = x1
    h2_ref[0] = (_rms(x1, fnw_ref[...]) * (1.0 + m[4:5]) + m[3:4]).astype(BF16)


def _mix_out(y_hy, h2dir, u_ml, x, mod, mod_row, ml_norm_w, w_out, norm_ffn_w):
    bsz, seq, d = x.shape
    d_ml = h2dir.shape[-1]
    tb = min(seq, 512)
    tok = lambda n, j=0: pl.BlockSpec((1, tb, n), lambda b, i: (b, i, j))
    hdir = lambda dr: pl.BlockSpec((1, 1, tb, d_ml), lambda b, i: (dr, b, i, 0))
    full = lambda a: pl.BlockSpec(a.shape, lambda b, i: (0,) * a.ndim)
    return pl.pallas_call(
        functools.partial(_mix_out_kernel, heads=ML_HEADS, dh=d_ml // ML_HEADS),
        grid=(bsz, seq // tb),
        in_specs=[tok(y_hy.shape[-1]), hdir(0), hdir(1), tok(d_ml, 3), tok(d),
                  pl.BlockSpec((1, 6, d), lambda b, i: (mod_row(b), 0, 0)),
                  full(ml_norm_w), full(w_out), full(norm_ffn_w)],
        out_specs=[tok(d), tok(d)],
        out_shape=[jax.ShapeDtypeStruct((bsz, seq, d), F32),
                   jax.ShapeDtypeStruct((bsz, seq, d), BF16)],
        compiler_params=_params("parallel", "parallel"),
        name="mixer_out",
    )(y_hy, h2dir, h2dir, u_ml, x, mod, ml_norm_w, w_out, norm_ffn_w)


def _ffn_kernel(*refs, width, halo, final, nj):
    if halo:
        (top_ref, main_ref, bot_ref, x1_ref, mod_ref, wup_ref, cw_ref, cb_ref, wd_ref, fw_ref,
         o_ref, acc_ref) = refs
    else:
        (main_ref, x1_ref, mod_ref, wup_ref, cw_ref, cb_ref, wd_ref, fw_ref, o_ref, acc_ref) = refs
    d = main_ref.shape[-1]
    hm = main_ref[...].reshape(-1, d)
    m_rows = hm.shape[0]
    acc_ref[...] = jnp.zeros_like(acc_ref)

    col = lax.broadcasted_iota(jnp.int32, (m_rows, 1), 0) % width
    not_first = (col != 0).astype(F32)
    not_last = (col != width - 1).astype(F32)
    if halo:
        r = pl.program_id(1)
        top_ok = jnp.where(r > 0, 1.0, 0.0)
        bot_ok = jnp.where(r < pl.num_programs(1) - 1, 1.0, 0.0)
        ht, hb = top_ref[0], bot_ref[0]

    def conv(w, cw, cb):
        um = jnp.dot(hm, w, preferred_element_type=F32)
        slabs = [(1, um)]
        if halo:
            ut = jnp.dot(ht, w, preferred_element_type=F32) * top_ok
            ub = jnp.dot(hb, w, preferred_element_type=F32) * bot_ok
            full = jnp.concatenate([ut, um, ub], axis=0)
            slabs = [(kr, full[kr * width:kr * width + m_rows]) for kr in range(3)]
        if halo:
            slabs = [(kr, s.astype(BF16)) for kr, s in slabs]
            cw = cw.astype(BF16)
        left = sum(s * cw[3 * kr:3 * kr + 1] for kr, s in slabs).astype(F32)
        mid = sum(s * cw[3 * kr + 1:3 * kr + 2] for kr, s in slabs).astype(F32)
        right = sum(s * cw[3 * kr + 2:3 * kr + 3] for kr, s in slabs).astype(F32)
        return (mid + not_first * pltpu.roll(left, 1, 0)
                + not_last * pltpu.roll(right, m_rows - 1, 0) + cb)

    def tile(j, carry):
        a = conv(wup_ref[j], cw_ref[j], cb_ref[j])
        val = conv(wup_ref[nj + j], cw_ref[nj + j], cb_ref[nj + j])
        act = (_silu(a) * val).astype(BF16)
        acc_ref[...] += jnp.dot(act, wd_ref[j], preferred_element_type=F32)
        return carry

    lax.fori_loop(0, nj, tile, 0)
    m = mod_ref[0]
    x2 = x1_ref[...].reshape(-1, d) + m[5:6] * acc_ref[...]
    if final:
        x2 = _rms(x2, fw_ref[...])
    o_ref[...] = x2.reshape(o_ref.shape)


def _ffn(h2, x1, mod, mod_row, w_up, conv_w, conv_b, w_down, final_w, rows, final):
    bsz, seq, d = x1.shape
    nj = w_down.shape[0]
    width = seq // rows
    halo = rows > 1
    if halo:
        rb = 16
        nb, m_tok, nr = 1, rb * width, rows // rb
    else:
        nb, m_tok, nr = math.gcd(bsz, 4), seq, 1
    tok = pl.BlockSpec((nb, m_tok, d), lambda b, r: (b, r, 0))
    whole = lambda a: pl.BlockSpec(a.shape, lambda b, r: (0,) * a.ndim, pipeline_mode=pl.Buffered(1))
    in_specs, args = [], []
    if halo:
        in_specs += [pl.BlockSpec((1, width, d), lambda b, r: (b, jnp.maximum(r * rb - 1, 0), 0)),
                     tok,
                     pl.BlockSpec((1, width, d), lambda b, r: (b, jnp.minimum((r + 1) * rb, rows - 1), 0))]
        args += [h2, h2, h2]
    else:
        in_specs += [tok]
        args += [h2]
    in_specs += [tok, pl.BlockSpec((1, 6, d), lambda b, r: (mod_row(b), 0, 0)),
                 whole(w_up), whole(conv_w), whole(conv_b), whole(w_down), whole(final_w)]
    args += [x1, mod, w_up, conv_w, conv_b, w_down, final_w]
    return pl.pallas_call(
        functools.partial(_ffn_kernel, width=width, halo=halo, final=final, nj=nj),
        grid=(bsz // nb, nr),
        in_specs=in_specs,
        out_specs=tok,
        out_shape=jax.ShapeDtypeStruct((bsz, seq, d), F32),
        scratch_shapes=[pltpu.VMEM((nb * m_tok, d), F32)],
        compiler_params=_params("parallel", "parallel"),
        name="conv_ffn",
    )(*args)


def _layer(x, mod, mod_row, p, tabs, c0, n0, m0, rows, final):
    bsz, seq, d = x.shape
    d_hy, d_ml = p["d_hy"], p["d_ml"]
    heads = ML_HEADS
    ctab, stab, sttab = tabs
    u_hy, u_ml, g = _inproj(x, mod, mod_row, p["norm_mix_w"], p["w_hy"], p["w_ml"], p["w_g"], p["b_g"])
    v, x0 = _hy_pre(u_hy, p["hy_conv_w"], p["hy_conv_b"])
    a, dd, ny = _filter_taps(seq, p["hy_filt"], d_hy)
    kre, kim = _filter_dft(ctab, stab, a, dd, ny)
    bt = math.gcd(bsz, 4 if seq <= DFT_M_TILE else 2)
    yre, yim = _dft_fwd(ctab, stab, v, kre, kim, bt)
    y_hy = _dft_inv(ctab, sttab, yre, yim, v, x0, p["hy_bias"], bt)
    qk = _ml_pre(u_ml, p["ml_conv_w"], p["ml_conv_b"], d_ml, d_ml // heads)
    h2dir, c_new, n_new, m_new = _mlstm(qk, u_ml, g, c0, n0, m0, d_ml)
    x1, h2 = _mix_out(y_hy, h2dir, u_ml, x, mod, mod_row, p["ml_norm_w"], p["w_out"], p["norm_ffn_w"])
    x2 = _ffn(h2, x1, mod, mod_row, p["ffn_w_up"], p["ffn_conv_w"], p["ffn_conv_b"],
              p["ffn_w_down"], p["final_norm_w"], rows, final)
    return x2, c_new, n_new, m_new


def _layer_params(l, d_hy, d_ml, norm_mix_w, w_in, b_gate, hy_conv_w, hy_conv_b, hy_filt, hy_bias,
                  ml_conv_w, ml_conv_b, ml_norm_w, w_out, norm_ffn_w, ffn_w_up, ffn_conv_w,
                  ffn_conv_b, ffn_w_down, final_norm_w):
    w = w_in[l]
    n_gates = b_gate.shape[-1]
    d_ff = ffn_w_down.shape[1]
    wg = jnp.pad(w[:, 3 * d_hy + 4 * d_ml:], ((0, 0), (0, LANE - n_gates)))
    nj = d_ff // FFN_TILE
    tiles = lambda a: jnp.transpose(a.reshape(a.shape[0], 2 * nj, FFN_TILE), (1, 0, 2))
    return {
        "d_hy": d_hy, "d_ml": d_ml,
        "norm_mix_w": norm_mix_w[l][None, :],
        "w_hy": w[:, :3 * d_hy].astype(BF16),
        "w_ml": w[:, 3 * d_hy:3 * d_hy + 4 * d_ml].astype(BF16),
        "w_g": jnp.concatenate(_split_hi_lo(wg), axis=1),
        "b_g": jnp.pad(b_gate[l], (0, LANE - n_gates))[None, :],
        "hy_conv_w": hy_conv_w[l], "hy_conv_b": hy_conv_b[l][None, :],
        "hy_filt": tuple(a[l] for a in hy_filt),
        "hy_bias": hy_bias[l][None, :],
        "ml_conv_w": ml_conv_w[l], "ml_conv_b": ml_conv_b[l][None, :],
        "ml_norm_w": ml_norm_w[l][None, :],
        "w_out": w_out[l].astype(BF16),
        "norm_ffn_w": norm_ffn_w[l][None, :],
        "ffn_w_up": tiles(ffn_w_up[l].astype(BF16)),
        "ffn_conv_w": tiles(ffn_conv_w[l].reshape(9, 2 * d_ff)),
        "ffn_conv_b": ffn_conv_b[l].reshape(2 * nj, 1, FFN_TILE),
        "ffn_w_down": ffn_w_down[l].astype(BF16).reshape(nj, FFN_TILE, -1),
        "final_norm_w": final_norm_w[None, :],
    }


def kernel(x_prompt, x_sample, state_mlstm_C, state_mlstm_n, state_mlstm_m, c, c_ctx, ada_w, ada_b, norm_mix_w, w_in, b_gate, hy_conv_w, hy_conv_b, hy_filt_w1, hy_filt_b1, hy_filt_w2, hy_filt_b2, hy_filt_w3, hy_filt_b3, hy_filt_w4, hy_freq, hy_bias, ml_conv_w, ml_conv_b, ml_norm_w, w_out, norm_ffn_w, ffn_w_up, ffn_conv_w, ffn_conv_b, ffn_w_down, final_norm_w):
    depth, d, _ = ada_w.shape
    bp, seq_p, _ = x_prompt.shape
    bs, seq_s, _ = x_sample.shape
    heads = ML_HEADS
    d_hy = hy_bias.shape[-1]
    d_ml = ml_norm_w.shape[-1]
    dh = d_ml // heads
    assert bs < MOD_ROWS and seq_s % GRID_W == 0

    cvec = jnp.concatenate([c, c_ctx[None, :], jnp.zeros((MOD_ROWS - bs - 1, d), F32)], axis=0)
    mod_all = _ada(cvec, ada_w, ada_b).reshape(depth, MOD_ROWS, 6, d)
    row_ctx = lambda b: bs
    row_lat = lambda b: b

    tabs_p = _dft_tables(seq_p)
    tabs_s = _dft_tables(seq_s)
    hy_filt = (hy_filt_w1, hy_filt_b1, hy_filt_w2, hy_filt_b2, hy_filt_w3, hy_filt_b3, hy_filt_w4, hy_freq)

    xp, xs = x_prompt, x_sample
    new_c, new_n, new_m = [], [], []
    for l in range(depth):
        p = _layer_params(l, d_hy, d_ml, norm_mix_w, w_in, b_gate, hy_conv_w, hy_conv_b, hy_filt,
                          hy_bias, ml_conv_w, ml_conv_b, ml_norm_w, w_out, norm_ffn_w, ffn_w_up,
                          ffn_conv_w, ffn_conv_b, ffn_w_down, final_norm_w)
        final = l == depth - 1
        xp, cl, nl, ml = _layer(xp, mod_all[l], row_ctx, p, tabs_p, None, None, None, 1, final)
        new_c.append(cl)
        new_n.append(nl)
        new_m.append(ml)
        xs, _, _, _ = _layer(xs, mod_all[l], row_lat, p, tabs_s, state_mlstm_C[:, l],
                             state_mlstm_n[:, l], state_mlstm_m[:, l], seq_s // GRID_W, final)
    return (xp, xs, jnp.stack(new_c, axis=1), jnp.stack(new_n, axis=1), jnp.stack(new_m, axis=1))
```

```python
import functools
import math

import numpy as np
import jax
import jax.numpy as jnp
from jax import lax
from jax.experimental import pallas as pl
from jax.experimental.pallas import tpu as pltpu

F32 = jnp.float32
BF16 = jnp.bfloat16
HIGHEST = lax.Precision.HIGHEST

GRID_W = 64
ML_HEADS = 4
CHUNK = 128
HY_BANDS = 16
HY_DECAY_TARGET = 1e-2
HY_FAST_DECAY_PCT = 0.3
HY_SLOW_DECAY_PCT = 1.5
HY_MIN_DECAY = math.log(HY_DECAY_TARGET) / HY_SLOW_DECAY_PCT
HY_MAX_DECAY = math.log(HY_DECAY_TARGET) / HY_FAST_DECAY_PCT
EPS = 1e-6

LANE = 128
HALO_ROWS = 16
VMEM_LIMIT = 56 * 1024 * 1024
MOD_ROWS = 8
DFT_ROW_TILE = 256
FFN_TILE = 256
DFT_M_TILE = 512


def _params(*sem):
    return pltpu.CompilerParams(dimension_semantics=sem, vmem_limit_bytes=VMEM_LIMIT)


def _silu(x):
    return x * jax.nn.sigmoid(x)


def _rms(x, w):
    return x * lax.rsqrt(jnp.mean(x * x, axis=-1, keepdims=True) + EPS) * w


def _split_hi_lo(x):
    hi = x.astype(BF16)
    return hi, (x - hi.astype(F32)).astype(BF16)


def _ada_kernel(c_ref, w_ref, b_ref, o_ref):
    s = _silu(c_ref[...])
    o_ref[0] = jnp.dot(s, w_ref[0], precision=HIGHEST, preferred_element_type=F32) + b_ref[0]


def _ada(cvec, ada_w, ada_b):
    depth, d, nmod = ada_w.shape
    tn = 1024
    return pl.pallas_call(
        _ada_kernel,
        grid=(depth, nmod // tn),
        in_specs=[
            pl.BlockSpec((MOD_ROWS, d), lambda l, j: (0, 0)),
            pl.BlockSpec((1, d, tn), lambda l, j: (l, 0, j)),
            pl.BlockSpec((1, 1, tn), lambda l, j: (l, 0, j)),
        ],
        out_specs=pl.BlockSpec((1, MOD_ROWS, tn), lambda l, j: (l, 0, j)),
        out_shape=jax.ShapeDtypeStruct((depth, MOD_ROWS, nmod), F32),
        compiler_params=_params("parallel", "parallel"),
        name="ada_mod",
    )(cvec, ada_w, ada_b.reshape(depth, 1, nmod))


def _inproj_kernel(x_ref, mod_ref, nw_ref, why_ref, wml_ref, wg_ref, bg_ref,
                   uhy_ref, uml_ref, g_ref):
    m = mod_ref[0]
    h = _rms(x_ref[0], nw_ref[...]) * (1.0 + m[1:2]) + m[0:1]
    h_hi, h_lo = _split_hi_lo(h)
    uhy_ref[0] = jnp.dot(h_hi, why_ref[...], preferred_element_type=F32).astype(uhy_ref.dtype)
    uml_ref[0] = jnp.dot(h_hi, wml_ref[...], preferred_element_type=F32).astype(uml_ref.dtype)
    wg = wg_ref[...]
    ng = g_ref.shape[-1]
    a = jnp.dot(h_hi, wg, preferred_element_type=F32)
    b = jnp.dot(h_lo, wg[:, :ng], preferred_element_type=F32)
    g_ref[0] = a[:, :ng] + a[:, ng:] + b + bg_ref[...]


def _inproj(x, mod, mod_row, nw, w_hy, w_ml, w_g, b_g):
    bsz, seq, d = x.shape
    tb = min(seq, 512)
    n_hy, n_ml = w_hy.shape[1], w_ml.shape[1]
    tok = lambda n: pl.BlockSpec((1, tb, n), lambda b, i: (b, i, 0))
    full = lambda a: pl.BlockSpec(a.shape, lambda b, i: (0,) * a.ndim)
    return pl.pallas_call(
        _inproj_kernel,
        grid=(bsz, seq // tb),
        in_specs=[tok(d), pl.BlockSpec((1, 6, d), lambda b, i: (mod_row(b), 0, 0)),
                  full(nw), full(w_hy), full(w_ml), full(w_g), full(b_g)],
        out_specs=[tok(n_hy), tok(n_ml), tok(LANE)],
        out_shape=[jax.ShapeDtypeStruct((bsz, seq, n_hy), BF16),
                   jax.ShapeDtypeStruct((bsz, seq, n_ml), BF16),
                   jax.ShapeDtypeStruct((bsz, seq, LANE), F32)],
        compiler_params=_params("parallel", "parallel"),
        name="inproj",
    )(x, mod, nw, w_hy, w_ml, w_g, b_g)


def _dwconv3(u, before, after, w, b):
    n = u.shape[0]
    row = lax.broadcasted_iota(jnp.int32, u.shape, 0)
    prev = jnp.where(row == 0, before, pltpu.roll(u, 1, 0))
    nxt = jnp.where(row == n - 1, after, pltpu.roll(u, n - 1, 0))
    return prev * w[0:1] + u * w[1:2] + nxt * w[2:3] + b


def _tile_neighbours(prev_ref, next_ref):
    i = pl.program_id(1)
    before = jnp.where(i == 0, 0.0, prev_ref[0].astype(F32)[HALO_ROWS - 1:HALO_ROWS])
    after = jnp.where(i == pl.num_programs(1) - 1, 0.0, next_ref[0].astype(F32)[0:1])
    return before, after


def _row_tile_specs(seq, tb, c, cblk=0):
    per = tb // HALO_ROWS
    last = seq // HALO_ROWS - 1
    main = pl.BlockSpec((1, tb, c), lambda b, i: (b, i, cblk))
    prev = pl.BlockSpec((1, HALO_ROWS, c), lambda b, i: (b, jnp.maximum(i * per - 1, 0), cblk))
    nxt = pl.BlockSpec((1, HALO_ROWS, c), lambda b, i: (b, jnp.minimum((i + 1) * per, last), cblk))
    return main, prev, nxt


def _hy_pre_kernel(u_ref, up_ref, un_ref, w_ref, b_ref, v_ref, x0_ref):
    before, after = _tile_neighbours(up_ref, un_ref)
    y = _dwconv3(u_ref[0].astype(F32), before, after, w_ref[...], b_ref[...])
    c = v_ref.shape[-1]
    x0_ref[0] = y[:, :c].astype(x0_ref.dtype)
    v_ref[0] = (y[:, 2 * c:] * y[:, c:2 * c]).astype(v_ref.dtype)


def _hy_pre(u_hy, conv_w, conv_b):
    bsz, seq, c3 = u_hy.shape
    tb = min(seq, 512)
    main, prev, nxt = _row_tile_specs(seq, tb, c3)
    full = lambda a: pl.BlockSpec(a.shape, lambda b, i: (0,) * a.ndim)
    out = pl.BlockSpec((1, tb, c3 // 3), lambda b, i: (b, i, 0))
    shp = jax.ShapeDtypeStruct((bsz, seq, c3 // 3), BF16)
    return pl.pallas_call(
        _hy_pre_kernel,
        grid=(bsz, seq // tb),
        in_specs=[main, prev, nxt, full(conv_w), full(conv_b)],
        out_specs=[out, out],
        out_shape=[shp, shp],
        compiler_params=_params("parallel", "parallel"),
        name="hyena_pre",
    )(u_hy, u_hy, u_hy, conv_w, conv_b)


def _ml_pre_kernel(u_ref, up_ref, un_ref, w_ref, b_ref, o_ref, *, d_ml, kscale):
    before, after = _tile_neighbours(up_ref, un_ref)
    y = _silu(_dwconv3(u_ref[0].astype(F32), before, after, w_ref[...], b_ref[...]))
    lane = lax.broadcasted_iota(jnp.int32, (1, y.shape[1]), 1)
    o_ref[0] = (y * jnp.where(lane >= d_ml, kscale, 1.0)).astype(o_ref.dtype)


def _ml_pre(u_ml, conv_w, conv_b, d_ml, dh):
    bsz, seq, _ = u_ml.shape
    tb = min(seq, 512)
    main, prev, nxt = _row_tile_specs(seq, tb, 2 * d_ml)
    full = lambda a: pl.BlockSpec(a.shape, lambda b, i: (0,) * a.ndim)
    return pl.pallas_call(
        functools.partial(_ml_pre_kernel, d_ml=d_ml, kscale=dh ** -0.5),
        grid=(bsz, seq // tb),
        in_specs=[main, prev, nxt, full(conv_w), full(conv_b)],
        out_specs=pl.BlockSpec((1, tb, 2 * d_ml), lambda b, i: (b, i, 0)),
        out_shape=jax.ShapeDtypeStruct((bsz, seq, 2 * d_ml), BF16),
        compiler_params=_params("parallel", "parallel"),
        name="mlstm_pre",
    )(u_ml, u_ml, u_ml, conv_w, conv_b)


def _dft_table_kernel(pre_ref, pim_ref, qre_ref, qim_ref, c_ref, s_ref, st_ref):
    pre, pim = pre_ref[0], pim_ref[0]
    qre, qim = qre_ref[...], qim_ref[...]
    c = pre * qre - pim * qim
    s = pre * qim + pim * qre
    grow = lax.broadcasted_iota(jnp.int32, c.shape, 0) + pl.program_id(0) * c.shape[0]
    lane = lax.broadcasted_iota(jnp.int32, c.shape, 1)
    alt_lane = (1 - 2 * (lane & 1)).astype(F32)
    alt_row = (1 - 2 * (grow & 1)).astype(F32)
    c_ref[...] = c.astype(BF16)
    s_ref[...] = jnp.where(grow == 0, alt_lane, s).astype(BF16)
    st_ref[...] = jnp.where(lane == 0, alt_row, s).astype(BF16)


def _dft_tables(seq):
    n = 2 * seq
    tm = DFT_ROW_TILE
    s = jnp.arange(seq, dtype=jnp.int32)[None, :]
    ang = lambda f: (-2.0 * math.pi / n) * ((f * s) % n).astype(F32)
    fp = (jnp.arange(seq // tm, dtype=jnp.int32) * tm)[:, None]
    fq = jnp.arange(tm, dtype=jnp.int32)[:, None]
    ap, aq = ang(fp), ang(fq)
    p3 = lambda a: a.reshape(seq // tm, 1, seq)
    pspec = pl.BlockSpec((1, 1, seq), lambda r: (r, 0, 0))
    qspec = pl.BlockSpec((tm, seq), lambda r: (0, 0))
    out = pl.BlockSpec((tm, seq), lambda r: (r, 0))
    shp = jax.ShapeDtypeStruct((seq, seq), BF16)
    return pl.pallas_call(
        _dft_table_kernel,
        grid=(seq // tm,),
        in_specs=[pspec, pspec, qspec, qspec],
        out_specs=[out, out, out],
        out_shape=[shp, shp, shp],
        compiler_params=_params("parallel"),
        name="dft_tables",
    )(p3(jnp.cos(ap)), p3(jnp.sin(ap)), jnp.cos(aq), jnp.sin(aq))


def _filter_kernel(z_ref, w1_ref, b1_ref, w2_ref, b2_ref, w3_ref, b3_ref, fr_ref,
                   w4_ref, dl_ref, a_ref, d_ref, ny_ref, h_ref):
    def dot3(x, w):
        x_hi, x_lo = _split_hi_lo(x)
        w_hi, w_lo = _split_hi_lo(w)
        mm = functools.partial(jnp.dot, preferred_element_type=F32)
        return mm(x_hi, w_hi) + mm(x_hi, w_lo) + mm(x_lo, w_hi)

    @pl.when(pl.program_id(0) == 0)
    def _():
        fr = fr_ref[...]
        h = jnp.sin(fr * (dot3(z_ref[...], w1_ref[...]) + b1_ref[...]))
        h = jnp.sin(fr * (dot3(h, w2_ref[...]) + b2_ref[...]))
        h_ref[...] = jnp.sin(fr * (dot3(h, w3_ref[...]) + b3_ref[...]))

    c = a_ref.shape[1]
    hfb = dot3(h_ref[...], w4_ref[0])
    decay = jnp.exp(-z_ref[:, 0:1] * dl_ref[...])
    hf = hfb[:, :c] * decay
    hb = hfb[:, c:] * decay
    row = lax.broadcasted_iota(jnp.int32, hf.shape, 0)
    hb = jnp.where(row == 0, 0.0, hb)
    inv = 1.0 / jnp.sum(jnp.abs(hf) + jnp.abs(hb), axis=0, keepdims=True)
    a = (hf + hb) * inv
    a_ref[...] = a.astype(BF16)
    d_ref[...] = ((hf - hb) * inv).astype(BF16)
    ny_ref[...] = jnp.sum(a * (1 - 2 * (row & 1)).astype(F32), axis=0, keepdims=True)


def _filter_taps(seq, filt, d_hy):
    w1, b1, w2, b2, w3, b3, w4, freq = filt
    emb, ffn = w1.shape
    t = jnp.linspace(0.0, 1.0, seq, dtype=F32)[:, None]
    wpos = (2.0 * math.pi / seq) * jnp.arange(seq, dtype=F32)[:, None]
    bands = jnp.linspace(1e-4, HY_BANDS - 1, HY_BANDS, dtype=F32)[None, :]
    z = jnp.concatenate([t, jnp.cos(bands * wpos), -jnp.sin(bands * wpos),
                         jnp.zeros((seq, LANE - emb), F32)], axis=-1)
    w1p = jnp.concatenate([w1, jnp.zeros((LANE - emb, ffn), F32)], axis=0)
    deltas = jnp.abs(jnp.linspace(HY_MIN_DECAY, HY_MAX_DECAY, d_hy, dtype=F32))[None, :]
    row = lambda a: a.reshape(1, -1)
    nblk = d_hy // LANE
    w4t = jnp.concatenate([w4[:, :d_hy].reshape(ffn, nblk, LANE), w4[:, d_hy:].reshape(ffn, nblk, LANE)], axis=2)
    w4t = jnp.transpose(w4t, (1, 0, 2))
    full = lambda a: pl.BlockSpec(a.shape, lambda j: (0,) * a.ndim)
    args = [z, w1p, row(b1), w2, row(b2), w3, row(b3), row(freq)]
    col = pl.BlockSpec((seq, LANE), lambda j: (0, j))
    vec = pl.BlockSpec((1, LANE), lambda j: (0, j))
    return pl.pallas_call(
        _filter_kernel,
        grid=(nblk,),
        in_specs=[full(a) for a in args] + [pl.BlockSpec((1, ffn, 2 * LANE), lambda j: (j, 0, 0)), vec],
        out_specs=[col, col, vec],
        out_shape=[jax.ShapeDtypeStruct((seq, d_hy), BF16),
                   jax.ShapeDtypeStruct((seq, d_hy), BF16),
                   jax.ShapeDtypeStruct((1, d_hy), F32)],
        scratch_shapes=[pltpu.VMEM((seq, ffn), F32)],
        compiler_params=_params("arbitrary"),
        name="hyena_filter",
    )(*args, w4t, deltas)


def _filter_dft_kernel(c_ref, s_ref, a_ref, d_ref, ny_ref, kre_ref, kim_ref):
    kre_ref[...] = jnp.dot(c_ref[...], a_ref[...], preferred_element_type=F32)
    kim = jnp.dot(s_ref[...], d_ref[...], preferred_element_type=F32)
    grow = lax.broadcasted_iota(jnp.int32, kim.shape, 0) + pl.program_id(0) * kim.shape[0]
    kim_ref[...] = jnp.where(grow == 0, ny_ref[...], kim)


def _filter_dft(ctab, stab, a, d, ny):
    seq, c = a.shape
    tm = min(seq, DFT_M_TILE)
    tab = pl.BlockSpec((tm, seq), lambda m: (m, 0))
    taps = pl.BlockSpec((seq, c), lambda m: (0, 0))
    return pl.pallas_call(
        _filter_dft_kernel,
        grid=(seq // tm,),
        in_specs=[tab, tab, taps, taps, pl.BlockSpec((1, c), lambda m: (0, 0))],
        out_specs=[pl.BlockSpec((tm, c), lambda m: (m, 0))] * 2,
        out_shape=[jax.ShapeDtypeStruct((seq, c), F32)] * 2,
        compiler_params=_params("parallel"),
        name="hyena_filter_dft",
    )(ctab, stab, a, d, ny)


def _dft_fwd_kernel(c_ref, s_ref, v_ref, kre_ref, kim_ref, yre_ref, yim_ref):
    kre, kim = kre_ref[...], kim_ref[...]
    grow = lax.broadcasted_iota(jnp.int32, kre.shape, 0) + pl.program_id(1) * kre.shape[0]
    packed = grow == 0
    for b in range(v_ref.shape[0]):
        vb = v_ref[b]
        xr = jnp.dot(c_ref[...], vb, preferred_element_type=F32)
        xi = jnp.dot(s_ref[...], vb, preferred_element_type=F32)
        yre = jnp.where(packed, 0.5 * xr * kre, xr * kre - xi * kim)
        yim = jnp.where(packed, 0.5 * xi * kim, xr * kim + xi * kre)
        yre_ref[b] = yre.astype(BF16)
        yim_ref[b] = yim.astype(BF16)


def _dft_fwd(ctab, stab, v, kre, kim, bt):
    bsz, seq, c = v.shape
    tm = min(seq, DFT_M_TILE)
    tab = pl.BlockSpec((tm, seq), lambda b, m: (m, 0))
    kspec = pl.BlockSpec((tm, c), lambda b, m: (m, 0))
    out = pl.BlockSpec((bt, tm, c), lambda b, m: (b, m, 0))
    shp = jax.ShapeDtypeStruct((bsz, seq, c), BF16)
    return pl.pallas_call(
        _dft_fwd_kernel,
        grid=(bsz // bt, seq // tm),
        in_specs=[tab, tab, pl.BlockSpec((bt, seq, c), lambda b, m: (b, 0, 0)), kspec, kspec],
        out_specs=[out, out],
        out_shape=[shp, shp],
        compiler_params=_params("parallel", "parallel"),
        name="hyena_dft_fwd",
    )(ctab, stab, v, kre, kim)


def _dft_inv_kernel(c_ref, st_ref, yre_ref, yim_ref, v_ref, x0_ref, hb_ref, o_ref, *, scale):
    for b in range(v_ref.shape[0]):
        conv = (jnp.dot(c_ref[...], yre_ref[b], preferred_element_type=F32)
                + jnp.dot(st_ref[...], yim_ref[b], preferred_element_type=F32))
        y = (conv * scale + hb_ref[...] * v_ref[b].astype(F32)) * x0_ref[b].astype(F32)
        o_ref[b] = y.astype(o_ref.dtype)


def _dft_inv(ctab, sttab, yre, yim, v, x0, hy_bias, bt):
    bsz, seq, c = v.shape
    tm = min(seq, DFT_M_TILE)
    tab = pl.BlockSpec((tm, seq), lambda b, m: (m, 0))
    yspec = pl.BlockSpec((bt, seq, c), lambda b, m: (b, 0, 0))
    tok = pl.BlockSpec((bt, tm, c), lambda b, m: (b, m, 0))
    return pl.pallas_call(
        functools.partial(_dft_inv_kernel, scale=1.0 / seq),
        grid=(bsz // bt, seq // tm),
        in_specs=[tab, tab, yspec, yspec, tok, tok, pl.BlockSpec((1, c), lambda b, m: (0, 0))],
        out_specs=tok,
        out_shape=jax.ShapeDtypeStruct((bsz, seq, c), BF16),
        compiler_params=_params("parallel", "parallel"),
        name="hyena_dft_inv",
    )(ctab, sttab, yre, yim, v, x0, hy_bias)


def _log_sigmoid(x):
    return jnp.minimum(x, 0.0) - jnp.log1p(jnp.exp(-jnp.abs(x)))


def _mlstm_kernel(*refs, heads, dh, zero_state):
    q_ref, k_ref, v_ref, g_ref, gt_ref = refs[:5]
    h_ref, c_out_ref, n_out_ref, m_out_ref, c_s, n_s, m_s = refs[-7:]
    step = pl.program_id(2)
    fwd = pl.program_id(0) == 0
    bt, t = q_ref.shape[0], q_ref.shape[1]

    @pl.when(step == 0)
    def _():
        if zero_state:
            c_s[...] = jnp.zeros_like(c_s)
            n_s[...] = jnp.zeros_like(n_s)
            m_s[...] = jnp.zeros_like(m_s)
        else:
            c0_ref, n0_ref, m0_ref = refs[5:8]
            c_s[...] = c0_ref[:, 0]
            n_s[...] = n0_ref[:, 0]
            m_s[...] = m0_ref[:, 0]

    row = lax.broadcasted_iota(jnp.int32, (t, t), 0)
    col = lax.broadcasted_iota(jnp.int32, (t, t), 1)
    sgn = jnp.where(fwd, 1, -1)
    seen = sgn * (row - col) >= 0
    seen_b = seen.astype(BF16)
    seen_tb = (sgn * (col - row) >= 0).astype(BF16)
    nt = (((1,), (1,)), ((), ()))
    tn = (((0,), (0,)), ((), ()))

    def split3(x, axis):
        hi = x.astype(BF16).astype(F32)
        r = x - hi
        mid = r.astype(BF16).astype(F32)
        return jnp.concatenate([hi, mid, r - mid], axis=axis).astype(BF16)

    g = g_ref[0, 0]
    gt = gt_ref[0, 0]
    lf = _log_sigmoid(g)
    lf_tot = jnp.sum(lf, axis=0, keepdims=True)
    cs = jnp.dot(seen_b, split3(lf, 1), preferred_element_type=F32)
    nl = lf.shape[1]
    bcol_all = cs[:, :nl] + cs[:, nl:2 * nl] + cs[:, 2 * nl:]
    rs = jnp.dot(split3(_log_sigmoid(gt), 0), seen_tb, preferred_element_type=F32)
    nr = gt.shape[0]
    brow_all = rs[:nr] + rs[nr:2 * nr] + rs[2 * nr:]
    chains = [(b, hd) for b in range(bt) for hd in range(heads)]
    sl = lambda hd: slice(hd * dh, (hd + 1) * dh)
    qs = [q_ref[b, :, sl(hd)] for b, hd in chains]
    ks = [k_ref[b, :, sl(hd)] for b, hd in chains]
    vs = [v_ref[b, :, sl(hd)].astype(BF16) for b, hd in chains]
    qk = [lax.dot_general(q, k, nt, preferred_element_type=F32) for q, k in zip(qs, ks)]
    each = lambda f, *ls: [f(*a) for a in zip(*ls)]
    fcols = [b * 2 * heads + heads + hd for b, hd in chains]
    bc = [bcol_all[:, c:c + 1] for c in fcols]
    rowt = [gt[c - heads:c - heads + 1, :] - brow_all[c:c + 1, :] for c in fcols]
    ic = [g[:, c - heads:c - heads + 1] for c in fcols]
    b_end = [lf_tot[:, c:c + 1] for c in fcols]
    m_prev = [m_s[b, hd:hd + 1, 0:1] for b, hd in chains]
    ct_prev = [c_s[b, hd] for b, hd in chains]
    n_prev = [n_s[b, hd:hd + 1, :] for b, hd in chains]
    dmat = each(lambda x, y: jnp.where(seen, x + y, -jnp.inf), bc, rowt)
    inter = each(lambda x, m: x + m, bc, m_prev)
    dmax = each(lambda d: jnp.max(d, axis=1, keepdims=True), dmat)
    m_t = each(jnp.maximum, inter, dmax)
    s = each(lambda r, d, m: r * jnp.exp(d - m), qk, dmat, m_t)
    w_inter = each(lambda i, m: jnp.exp(i - m), inter, m_t)
    qf = each(lambda q: q.astype(F32), qs)
    lhs = each(lambda x, q, w: jnp.concatenate([x.astype(BF16), (q * w).astype(BF16)], axis=1),
               s, qf, w_inter)
    rhs = each(lambda v, c: jnp.concatenate([v, c.astype(BF16)], axis=0), vs, ct_prev)
    nums = each(lambda l, r: jnp.dot(l, r, preferred_element_type=F32), lhs, rhs)
    qn = each(lambda q, n: jnp.sum(q * n, axis=1, keepdims=True), qf, n_prev)
    ssum = each(lambda x: jnp.sum(x, axis=1, keepdims=True), s)
    den = each(lambda a, w, x: a + w * x, ssum, w_inter, qn)
    w_log = each(lambda e, x, i: e - x + i, b_end, bc, ic)
    wmax = each(lambda w: jnp.max(w, axis=0, keepdims=True), w_log)
    m_new = each(lambda e, m, w: jnp.maximum(e + m, w), b_end, m_prev, wmax)
    w_s = each(lambda w, m: jnp.exp(w - m), w_log, m_new)
    w_c = each(lambda e, m, mn: jnp.exp(e + m - mn), b_end, m_prev, m_new)
    kw = each(lambda k, w: k.astype(F32) * w, ks, w_s)
    upds = each(lambda x, v: lax.dot_general(x.astype(BF16), v, tn, preferred_element_type=F32), kw, vs)
    ksum = each(lambda x: jnp.sum(x, axis=0, keepdims=True), kw)
    for i, (b, hd) in enumerate(chains):
        h_ref[0, b, :, sl(hd)] = nums[i] / jnp.maximum(jnp.abs(den[i]), jnp.exp(-m_t[i]))
        c_s[b, hd] = w_c[i] * ct_prev[i] + upds[i]
        n_s[b, hd:hd + 1, :] = w_c[i] * n_prev[i] + ksum[i]
        m_s[b, hd:hd + 1, :] = jnp.broadcast_to(m_new[i], (1, dh))

    @pl.when(step == pl.num_programs(2) - 1)
    def _():
        c_out_ref[:, 0] = c_s[...]
        n_out_ref[:, 0] = n_s[...]
        m_out_ref[:, 0] = m_s[...]


def _mlstm(qk, u_ml, g, c0, n0, m0, d_ml):
    bsz, seq, _ = qk.shape
    heads = ML_HEADS
    dh = d_ml // heads
    nc = seq // CHUNK
    bt = math.gcd(bsz, 4)
    nbb = bsz // bt
    gi = g[..., :4 * heads].reshape(nbb, bt, seq, 2, 2 * heads)
    g2 = jnp.transpose(gi, (3, 0, 2, 1, 4)).reshape(2, nbb, seq, bt * 2 * heads)
    gt2 = jnp.swapaxes(g2, 2, 3)
    g2 = jnp.pad(g2, ((0, 0), (0, 0), (0, 0), (0, LANE - bt * 2 * heads)))
    chunk = lambda d, c: c + d * (nc - 1 - 2 * c)
    tok = lambda j: pl.BlockSpec((bt, CHUNK, d_ml), lambda d, b, c: (b, chunk(d, c), j))
    st = lambda *tail: pl.BlockSpec((bt, 1, heads) + tail, lambda d, b, c: (b, d, 0) + (0,) * len(tail))
    zero_state = c0 is None
    states = [] if zero_state else [jnp.swapaxes(c0, -1, -2), n0,
                                    jnp.broadcast_to(m0[..., None], m0.shape + (dh,))]
    h, ct, n, m = pl.pallas_call(
        functools.partial(_mlstm_kernel, heads=heads, dh=dh, zero_state=zero_state),
        grid=(2, nbb, nc),
        in_specs=[tok(0), tok(1), tok(2),
                  pl.BlockSpec((1, 1, CHUNK, LANE), lambda d, b, c: (d, b, chunk(d, c), 0)),
                  pl.BlockSpec((1, 1, bt * 2 * heads, CHUNK), lambda d, b, c: (d, b, 0, chunk(d, c)))]
                 + ([] if zero_state else [st(dh, dh), st(dh), st(dh)]),
        out_specs=[pl.BlockSpec((1, bt, CHUNK, d_ml), lambda d, b, c: (d, b, chunk(d, c), 0)),
                   st(dh, dh), st(dh), st(dh)],
        out_shape=[jax.ShapeDtypeStruct((2, bsz, seq, d_ml), F32),
                   jax.ShapeDtypeStruct((bsz, 2, heads, dh, dh), F32),
                   jax.ShapeDtypeStruct((bsz, 2, heads, dh), F32),
                   jax.ShapeDtypeStruct((bsz, 2, heads, dh), F32)],
        scratch_shapes=[pltpu.VMEM((bt, heads, dh, dh), F32), pltpu.VMEM((bt, heads, dh), F32),
                        pltpu.VMEM((bt, heads, dh), F32)],
        compiler_params=_params("parallel", "parallel", "arbitrary"),
        name="mlstm_scan",
    )(qk, qk, u_ml, g2, gt2, *states)
    return h, jnp.swapaxes(ct, -1, -2), n, m[..., 0]


def _mix_out_kernel(yhy_ref, hf_ref, hb_ref, o_ref, x_ref, mod_ref, mlw_ref, wout_ref, fnw_ref,
                    x1_ref, h2_ref, *, heads, dh):
    m = mod_ref[0]
    h = hf_ref[0, 0] + hb_ref[0, 0]
    o = o_ref[0].astype(F32)
    mlw = mlw_ref[...]
    parts = [yhy_ref[0].astype(BF16)]
    for hd in range(heads):
        sl = slice(hd * dh, (hd + 1) * dh)
        parts.append((_rms(h[:, sl], mlw[:, sl]) * jax.nn.sigmoid(o[:, sl])).astype(BF16))
    y = jnp.concatenate(parts, axis=-1)
    x1 = x_ref[0] + m[2:3] * jnp.dot(y, wout_ref[...], preferred_element_type=F32)
    x1_ref[0] = x1
    h2_ref[0] = (_rms(x1, fnw_ref[...]) * (1.0 + m[4:5]) + m[3:4]).astype(BF16)


def _mix_out(y_hy, h2dir, u_ml, x, mod, mod_row, ml_norm_w, w_out, norm_ffn_w):
    bsz, seq, d = x.shape
    d_ml = h2dir.shape[-1]
    tb = min(seq, 512)
    tok = lambda n, j=0: pl.BlockSpec((1, tb, n), lambda b, i: (b, i, j))
    hdir = lambda dr: pl.BlockSpec((1, 1, tb, d_ml), lambda b, i: (dr, b, i, 0))
    full = lambda a: pl.BlockSpec(a.shape, lambda b, i: (0,) * a.ndim)
    return pl.pallas_call(
        functools.partial(_mix_out_kernel, heads=ML_HEADS, dh=d_ml // ML_HEADS),
        grid=(bsz, seq // tb),
        in_specs=[tok(y_hy.shape[-1]), hdir(0), hdir(1), tok(d_ml, 3), tok(d),
                  pl.BlockSpec((1, 6, d), lambda b, i: (mod_row(b), 0, 0)),
                  full(ml_norm_w), full(w_out), full(norm_ffn_w)],
        out_specs=[tok(d), tok(d)],
        out_shape=[jax.ShapeDtypeStruct((bsz, seq, d), F32),
                   jax.ShapeDtypeStruct((bsz, seq, d), BF16)],
        compiler_params=_params("parallel", "parallel"),
        name="mixer_out",
    )(y_hy, h2dir, h2dir, u_ml, x, mod, ml_norm_w, w_out, norm_ffn_w)


def _ffn_kernel(*refs, width, halo, final, nj):
    if halo:
        (top_ref, main_ref, bot_ref, x1_ref, mod_ref, wup_ref, cw_ref, cb_ref, wd_ref, fw_ref,
         o_ref, acc_ref) = refs
    else:
        (main_ref, x1_ref, mod_ref, wup_ref, cw_ref, cb_ref, wd_ref, fw_ref, o_ref, acc_ref) = refs
    d = main_ref.shape[-1]
    hm = main_ref[...].reshape(-1, d)
    m_rows = hm.shape[0]
    acc_ref[...] = jnp.zeros_like(acc_ref)

    col = lax.broadcasted_iota(jnp.int32, (m_rows, 1), 0) % width
    not_first = (col != 0).astype(F32)
    not_last = (col != width - 1).astype(F32)
    if halo:
        r = pl.program_id(1)
        top_ok = jnp.where(r > 0, 1.0, 0.0)
        bot_ok = jnp.where(r < pl.num_programs(1) - 1, 1.0, 0.0)
        ht, hb = top_ref[0], bot_ref[0]

    def conv(w, cw, cb):
        um = jnp.dot(hm, w, preferred_element_type=F32)
        slabs = [(1, um)]
        if halo:
            ut = jnp.dot(ht, w, preferred_element_type=F32) * top_ok
            ub = jnp.dot(hb, w, preferred_element_type=F32) * bot_ok
            full = jnp.concatenate([ut, um, ub], axis=0)
            slabs = [(kr, full[kr * width:kr * width + m_rows]) for kr in range(3)]
        if halo:
            slabs = [(kr, s.astype(BF16)) for kr, s in slabs]
            cw = cw.astype(BF16)
        left = sum(s * cw[3 * kr:3 * kr + 1] for kr, s in slabs).astype(F32)
        mid = sum(s * cw[3 * kr + 1:3 * kr + 2] for kr, s in slabs).astype(F32)
        right = sum(s * cw[3 * kr + 2:3 * kr + 3] for kr, s in slabs).astype(F32)
        return (mid + not_first * pltpu.roll(left, 1, 0)
                + not_last * pltpu.roll(right, m_rows - 1, 0) + cb)

    def tile(j, carry):
        a = conv(wup_ref[j], cw_ref[j], cb_ref[j])
        val = conv(wup_ref[nj + j], cw_ref[nj + j], cb_ref[nj + j])
        act = (_silu(a) * val).astype(BF16)
        acc_ref[...] += jnp.dot(act, wd_ref[j], preferred_element_type=F32)
        return carry

    lax.fori_loop(0, nj, tile, 0)
    m = mod_ref[0]
    x2 = x1_ref[...].reshape(-1, d) + m[5:6] * acc_ref[...]
    if final:
        x2 = _rms(x2, fw_ref[...])
    o_ref[...] = x2.reshape(o_ref.shape)


def _ffn(h2, x1, mod, mod_row, w_up, conv_w, conv_b, w_down, final_w, rows, final):
    bsz, seq, d = x1.shape
    nj = w_down.shape[0]
    width = seq // rows
    halo = rows > 1
    if halo:
        rb = 16
        nb, m_tok, nr = 1, rb * width, rows // rb
    else:
        nb, m_tok, nr = math.gcd(bsz, 4), seq, 1
    tok = pl.BlockSpec((nb, m_tok, d), lambda b, r: (b, r, 0))
    whole = lambda a: pl.BlockSpec(a.shape, lambda b, r: (0,) * a.ndim, pipeline_mode=pl.Buffered(1))
    in_specs, args = [], []
    if halo:
        in_specs += [pl.BlockSpec((1, width, d), lambda b, r: (b, jnp.maximum(r * rb - 1, 0), 0)),
                     tok,
                     pl.BlockSpec((1, width, d), lambda b, r: (b, jnp.minimum((r + 1) * rb, rows - 1), 0))]
        args += [h2, h2, h2]
    else:
        in_specs += [tok]
        args += [h2]
    in_specs += [tok, pl.BlockSpec((1, 6, d), lambda b, r: (mod_row(b), 0, 0)),
                 whole(w_up), whole(conv_w), whole(conv_b), whole(w_down), whole(final_w)]
    args += [x1, mod, w_up, conv_w, conv_b, w_down, final_w]
    return pl.pallas_call(
        functools.partial(_ffn_kernel, width=width, halo=halo, final=final, nj=nj),
        grid=(bsz // nb, nr),
        in_specs=in_specs,
        out_specs=tok,
        out_shape=jax.ShapeDtypeStruct((bsz, seq, d), F32),
        scratch_shapes=[pltpu.VMEM((nb * m_tok, d), F32)],
        compiler_params=_params("parallel", "parallel"),
        name="conv_ffn",
    )(*args)


def _layer(x, mod, mod_row, p, tabs, c0, n0, m0, rows, final):
    bsz, seq, d = x.shape
    d_hy, d_ml = p["d_hy"], p["d_ml"]
    heads = ML_HEADS
    ctab, stab, sttab = tabs
    u_hy, u_ml, g = _inproj(x, mod, mod_row, p["norm_mix_w"], p["w_hy"], p["w_ml"], p["w_g"], p["b_g"])
    v, x0 = _hy_pre(u_hy, p["hy_conv_w"], p["hy_conv_b"])
    a, dd, ny = _filter_taps(seq, p["hy_filt"], d_hy)
    kre, kim = _filter_dft(ctab, stab, a, dd, ny)
    bt = math.gcd(bsz, 4 if seq <= DFT_M_TILE else 2)
    yre, yim = _dft_fwd(ctab, stab, v, kre, kim, bt)
    y_hy = _dft_inv(ctab, sttab, yre, yim, v, x0, p["hy_bias"], bt)
    qk = _ml_pre(u_ml, p["ml_conv_w"], p["ml_conv_b"], d_ml, d_ml // heads)
    h2dir, c_new, n_new, m_new = _mlstm(qk, u_ml, g, c0, n0, m0, d_ml)
    x1, h2 = _mix_out(y_hy, h2dir, u_ml, x, mod, mod_row, p["ml_norm_w"], p["w_out"], p["norm_ffn_w"])
    x2 = _ffn(h2, x1, mod, mod_row, p["ffn_w_up"], p["ffn_conv_w"], p["ffn_conv_b"],
              p["ffn_w_down"], p["final_norm_w"], rows, final)
    return x2, c_new, n_new, m_new


def _layer_params(l, d_hy, d_ml, norm_mix_w, w_in, b_gate, hy_conv_w, hy_conv_b, hy_filt, hy_bias,
                  ml_conv_w, ml_conv_b, ml_norm_w, w_out, norm_ffn_w, ffn_w_up, ffn_conv_w,
                  ffn_conv_b, ffn_w_down, final_norm_w):
    w = w_in[l]
    n_gates = b_gate.shape[-1]
    d_ff = ffn_w_down.shape[1]
    wg = jnp.pad(w[:, 3 * d_hy + 4 * d_ml:], ((0, 0), (0, LANE - n_gates)))
    nj = d_ff // FFN_TILE
    tiles = lambda a: jnp.transpose(a.reshape(a.shape[0], 2 * nj, FFN_TILE), (1, 0, 2))
    return {
        "d_hy": d_hy, "d_ml": d_ml,
        "norm_mix_w": norm_mix_w[l][None, :],
        "w_hy": w[:, :3 * d_hy].astype(BF16),
        "w_ml": w[:, 3 * d_hy:3 * d_hy + 4 * d_ml].astype(BF16),
        "w_g": jnp.concatenate(_split_hi_lo(wg), axis=1),
        "b_g": jnp.pad(b_gate[l], (0, LANE - n_gates))[None, :],
        "hy_conv_w": hy_conv_w[l], "hy_conv_b": hy_conv_b[l][None, :],
        "hy_filt": tuple(a[l] for a in hy_filt),
        "hy_bias": hy_bias[l][None, :],
        "ml_conv_w": ml_conv_w[l], "ml_conv_b": ml_conv_b[l][None, :],
        "ml_norm_w": ml_norm_w[l][None, :],
        "w_out": w_out[l].astype(BF16),
        "norm_ffn_w": norm_ffn_w[l][None, :],
        "ffn_w_up": tiles(ffn_w_up[l].astype(BF16)),
        "ffn_conv_w": tiles(ffn_conv_w[l].reshape(9, 2 * d_ff)),
        "ffn_conv_b": ffn_conv_b[l].reshape(2 * nj, 1, FFN_TILE),
        "ffn_w_down": ffn_w_down[l].astype(BF16).reshape(nj, FFN_TILE, -1),
        "final_norm_w": final_norm_w[None, :],
    }


def kernel(x_prompt, x_sample, state_mlstm_C, state_mlstm_n, state_mlstm_m, c, c_ctx, ada_w, ada_b, norm_mix_w, w_in, b_gate, hy_conv_w, hy_conv_b, hy_filt_w1, hy_filt_b1, hy_filt_w2, hy_filt_b2, hy_filt_w3, hy_filt_b3, hy_filt_w4, hy_freq, hy_bias, ml_conv_w, ml_conv_b, ml_norm_w, w_out, norm_ffn_w, ffn_w_up, ffn_conv_w, ffn_conv_b, ffn_w_down, final_norm_w):
    depth, d, _ = ada_w.shape
    bp, seq_p, _ = x_prompt.shape
    bs, seq_s, _ = x_sample.shape
    heads = ML_HEADS
    d_hy = hy_bias.shape[-1]
    d_ml = ml_norm_w.shape[-1]
    dh = d_ml // heads
    assert bs < MOD_ROWS and seq_s % GRID_W == 0

    cvec = jnp.concatenate([c, c_ctx[None, :], jnp.zeros((MOD_ROWS - bs - 1, d), F32)], axis=0)
    mod_all = _ada(cvec, ada_w, ada_b).reshape(depth, MOD_ROWS, 6, d)
    row_ctx = lambda b: bs
    row_lat = lambda b: b

    tabs_p = _dft_tables(seq_p)
    tabs_s = _dft_tables(seq_s)
    hy_filt = (hy_filt_w1, hy_filt_b1, hy_filt_w2, hy_filt_b2, hy_filt_w3, hy_filt_b3, hy_filt_w4, hy_freq)

    xp, xs = x_prompt, x_sample
    new_c, new_n, new_m = [], [], []
    for l in range(depth):
        p = _layer_params(l, d_hy, d_ml, norm_mix_w, w_in, b_gate, hy_conv_w, hy_conv_b, hy_filt,
                          hy_bias, ml_conv_w, ml_conv_b, ml_norm_w, w_out, norm_ffn_w, ffn_w_up,
                          ffn_conv_w, ffn_conv_b, ffn_w_down, final_norm_w)
        final = l == depth - 1
        xp, cl, nl, ml = _layer(xp, mod_all[l], row_ctx, p, tabs_p, None, None, None, 1, final)
        new_c.append(cl)
        new_n.append(nl)
        new_m.append(ml)
        xs, _, _, _ = _layer(xs, mod_all[l], row_lat, p, tabs_s, state_mlstm_C[:, l],
                             state_mlstm_n[:, l], state_mlstm_m[:, l], seq_s // GRID_W, final)
    return (xp, xs, jnp.stack(new_c, axis=1), jnp.stack(new_n, axis=1), jnp.stack(new_m, axis=1))
```
